```python
import math
import jax, jax.numpy as jnp
from jax import lax
import numpy as np

D_MODEL = 1024
BATCH = 4
SEQ = 4096
DEPTH = 1

HEAD_DIM = 64
N_HEADS_MOBA = 8
N_HEADS_RWKV = 8
W_MOBA = N_HEADS_MOBA * HEAD_DIM
W_RWKV = N_HEADS_RWKV * HEAD_DIM
MOBA_BLOCK = 256
MOBA_TOPK = 3
Q_CHUNK = 32
NUM_BUCKETS = 32
MAX_DISTANCE = 1024
D_FF = 2816
DECAY_LORA = 64
AAA_LORA = 64
GATE_LORA = 160
RMS_EPS = 1e-6
GN_EPS = HEAD_DIM * 1e-5
IN_COLS = 3 * W_MOBA + 3 * W_RWKV + 2 * D_MODEL

kernel_name = "hybrid_moba_rwkv7_macaron"


def rms_norm(x, g):
    xf = x.astype(jnp.float32)
    y = xf * lax.rsqrt(jnp.mean(xf * xf, axis=-1, keepdims=True) + RMS_EPS)
    return (y * g.astype(jnp.float32)).astype(x.dtype)


def swiglu(h, w_gate, w_up, w_down):
    return (jax.nn.silu(h @ w_gate) * (h @ w_up)) @ w_down


def t5_bucket(dist):
    n = jnp.maximum(dist, 0)
    max_exact = NUM_BUCKETS // 2
    nf = jnp.maximum(n, max_exact).astype(jnp.float32)
    large = max_exact + (jnp.log(nf / max_exact) / math.log(MAX_DISTANCE / max_exact)
                         * (NUM_BUCKETS - max_exact)).astype(jnp.int32)
    large = jnp.minimum(large, NUM_BUCKETS - 1)
    return jnp.where(n < max_exact, n, large)


def shift_seq(z):
    return jnp.pad(z, ((0, 0), (1, 0), (0, 0)))[:, :-1]


def moba_attention(q, k, v, rel_bias):
    B, S, H, D = q.shape
    nb = -(-S // MOBA_BLOCK)
    s_pad = nb * MOBA_BLOCK
    pad = ((0, 0), (0, 0), (0, s_pad - S), (0, 0))
    q = jnp.pad(q.transpose(0, 2, 1, 3), pad)
    k = jnp.pad(k.transpose(0, 2, 1, 3), pad)
    v = jnp.pad(v.transpose(0, 2, 1, 3), pad)
    k_blocks = k.reshape(B, H, nb, MOBA_BLOCK, D)
    v_blocks = v.reshape(B, H, nb, MOBA_BLOCK, D)
    k_mean = jnp.mean(k_blocks.astype(jnp.float32), axis=3)
    topk = min(MOBA_TOPK, max(nb - 1, 1))
    n_chunks = s_pad // Q_CHUNK
    q_chunks = q.reshape(B, H, n_chunks, Q_CHUNK, D).transpose(2, 0, 1, 3, 4)
    bias_t = rel_bias.T
    head_ids = jnp.arange(H)[None, :, None, None, None]
    scale = D ** -0.5
    gather = jax.vmap(jax.vmap(lambda blocks, ix: blocks[ix]))
    key_off = jnp.arange(MOBA_BLOCK)

    def chunk(args):
        q_c, c = args
        start = c * Q_CHUNK
        t = start + jnp.arange(Q_CHUNK)
        blk = start // MOBA_BLOCK
        gate = jnp.einsum('bhqd,bhnd->bhqn', q_c.astype(jnp.float32), k_mean)
        gate = jnp.where(jnp.arange(nb) < blk, gate, -jnp.inf)
        _, idx = lax.top_k(gate, topk)
        valid = idx < blk
        k_sel = gather(k_blocks, idx)
        v_sel = gather(v_blocks, idx)
        kpos_sel = idx[..., None] * MOBA_BLOCK + key_off
        bias_sel = bias_t[head_ids, t5_bucket(t[:, None, None] - kpos_sel)]
        s_sel = (jnp.einsum('bhqd,bhqmkd->bhqmk', q_c, k_sel).astype(jnp.float32) * scale
                 + bias_sel.astype(jnp.float32))
        s_sel = jnp.where(valid[..., None], s_sel, -jnp.inf)
        k_own = lax.dynamic_index_in_dim(k_blocks, blk, axis=2, keepdims=False)
        v_own = lax.dynamic_index_in_dim(v_blocks, blk, axis=2, keepdims=False)
        dist_own = t[:, None] - (blk * MOBA_BLOCK + key_off)[None, :]
        bias_own = bias_t[:, t5_bucket(dist_own)]
        s_own = (jnp.einsum('bhqd,bhkd->bhqk', q_c, k_own).astype(jnp.float32) * scale
                 + bias_own.astype(jnp.float32))
        s_own = jnp.where(dist_own >= 0, s_own, -jnp.inf)
        logits = jnp.concatenate(
            [s_sel.reshape(B, H, Q_CHUNK, topk * MOBA_BLOCK), s_own], axis=-1)
        p = jax.nn.softmax(logits, axis=-1).astype(v.dtype)
        p_sel = p[..., :topk * MOBA_BLOCK].reshape(B, H, Q_CHUNK, topk, MOBA_BLOCK)
        p_own = p[..., topk * MOBA_BLOCK:]
        return (jnp.einsum('bhqmk,bhqmkd->bhqd', p_sel, v_sel)
                + jnp.einsum('bhqk,bhkd->bhqd', p_own, v_own))

    out = lax.map(chunk, (q_chunks, jnp.arange(n_chunks)))
    out = out.transpose(1, 0, 3, 2, 4).reshape(B, s_pad, H, D)[:, :S]
    return out.reshape(B, S, H * D)


def _rwkv_step(state, inp):
    r_t, w_t, k_t, v_t, a_t, b_t = inp
    sa = jnp.einsum('bhvk,bhk->bhv', state, a_t)
    state = (state * w_t[:, :, None, :] + sa[..., None] * b_t[:, :, None, :]
             + v_t[..., None] * k_t[:, :, None, :])
    y = jnp.einsum('bhvk,bhk->bhv', state, r_t)
    return state, y


def rwkv7_time_mix(h, r_p, k_p, v_p, mix_rkv, mix_wag, w_lora_a, w_lora_b, w0,
                   a_lora_a, a_lora_b, a0, g_lora_a, g_lora_b, k_k, k_a, r_k,
                   ln_x_w, ln_x_b):
    B, S, _ = h.shape
    H, N = N_HEADS_RWKV, HEAD_DIM
    r = r_p + (shift_seq(r_p) - r_p) * mix_rkv[0]
    k = k_p + (shift_seq(k_p) - k_p) * mix_rkv[1]
    v = v_p + (shift_seq(v_p) - v_p) * mix_rkv[2]
    dh = shift_seq(h) - h
    xw = h + dh * mix_wag[0]
    xa = h + dh * mix_wag[1]
    xg = h + dh * mix_wag[2]
    w_raw = -jax.nn.softplus(-(w0 + jnp.tanh(xw @ w_lora_a) @ w_lora_b)) - 0.5
    decay = jnp.exp(-jnp.exp(w_raw.astype(jnp.float32)))
    a = jax.nn.sigmoid(a0 + (xa @ a_lora_a) @ a_lora_b)
    g = jax.nn.sigmoid(xg @ g_lora_a) @ g_lora_b
    kk = (k * k_k).reshape(B, S, H, N).astype(jnp.float32)
    kk = kk / jnp.maximum(jnp.linalg.norm(kk, axis=-1, keepdims=True), 1e-12)
    k = k * (1.0 + (a - 1.0) * k_a)
    heads = lambda z: z.reshape(B, S, H, N).astype(jnp.float32)
    r_h, k_h, v_h, a_h, w_h = heads(r), heads(k), heads(v), heads(a), heads(decay)
    tb = lambda z: z.transpose(1, 0, 2, 3)
    state0 = jnp.zeros((B, H, N, N), jnp.float32)
    _, y = lax.scan(_rwkv_step, state0,
                    (tb(r_h), tb(w_h), tb(k_h), tb(v_h), tb(-kk), tb(kk * a_h)))
    y = y.transpose(1, 0, 2, 3)
    mu = jnp.mean(y, axis=-1, keepdims=True)
    var = jnp.mean(jnp.square(y - mu), axis=-1, keepdims=True)
    y = ((y - mu) * lax.rsqrt(var + GN_EPS)).reshape(B, S, H * N)
    y = y * ln_x_w.astype(jnp.float32) + ln_x_b.astype(jnp.float32)
    bonus = jnp.sum(r_h * k_h * r_k.astype(jnp.float32), axis=-1, keepdims=True) * v_h
    y = (y + bonus.reshape(B, S, H * N)).astype(h.dtype)
    return y * g


def _dense(k, shape, fan_in, scale=1.0):
    return jax.random.normal(k, shape, jnp.float32) * (scale * fan_in ** -0.5)


def setup_inputs(seed: int = 0) -> dict:
    key = jax.random.key(seed)
    ks = iter(jax.random.split(key, 48))
    L, D = DEPTH, D_MODEL
    nrm = lambda shape, s: s * jax.random.normal(next(ks), shape, jnp.float32)
    uni = lambda shape, lo, hi: jax.random.uniform(next(ks), shape, jnp.float32, lo, hi)
    return {
        "x": jax.random.normal(next(ks), (BATCH, SEQ, D), jnp.float32),
        "rel_bias": nrm((NUM_BUCKETS, N_HEADS_MOBA), 0.1),
        "norm_ffn1": 1.0 + nrm((L, D), 0.05),
        "ffn1_gate": _dense(next(ks), (L, D, D_FF), D),
        "ffn1_up": _dense(next(ks), (L, D, D_FF), D),
        "ffn1_down": _dense(next(ks), (L, D_FF, D), D_FF),
        "norm_mix": 1.0 + nrm((L, D), 0.05),
        "w_in": _dense(next(ks), (L, D, IN_COLS), D),
        "mix_rkv": uni((L, 3, W_RWKV), 0.0, 1.0),
        "mix_wag": uni((L, 3, D), 0.0, 1.0),
        "w_lora_a": _dense(next(ks), (L, D, DECAY_LORA), D),
        "w_lora_b": _dense(next(ks), (L, DECAY_LORA, W_RWKV), DECAY_LORA, 0.1),
        "w0": uni((L, W_RWKV), -6.5, -1.5),
        "a_lora_a": _dense(next(ks), (L, D, AAA_LORA), D),
        "a_lora_b": _dense(next(ks), (L, AAA_LORA, W_RWKV), AAA_LORA, 0.1),
        "a0": nrm((L, W_RWKV), 0.1),
        "g_lora_a": _dense(next(ks), (L, D, GATE_LORA), D),
        "g_lora_b": _dense(next(ks), (L, GATE_LORA, W_RWKV), GATE_LORA),
        "k_k": 0.85 + nrm((L, W_RWKV), 0.05),
        "k_a": 1.0 + nrm((L, W_RWKV), 0.05),
        "r_k": nrm((L, N_HEADS_RWKV, HEAD_DIM), 0.1),
        "ln_x_w": 1.0 + nrm((L, W_RWKV), 0.05),
        "ln_x_b": nrm((L, W_RWKV), 0.02),
        "proj_a": _dense(next(ks), (L, W_MOBA, D), W_MOBA),
        "proj_b": _dense(next(ks), (L, W_RWKV, D), W_RWKV),
        "w_out": _dense(next(ks), (L, D, D), D),
        "norm_ffn2": 1.0 + nrm((L, D), 0.05),
        "ffn2_gate": _dense(next(ks), (L, D, D_FF), D),
        "ffn2_up": _dense(next(ks), (L, D, D_FF), D),
        "ffn2_down": _dense(next(ks), (L, D_FF, D), D_FF),
        "norm_final": 1.0 + nrm((D,), 0.05),
    }


def reference(x, rel_bias, norm_ffn1, ffn1_gate, ffn1_up, ffn1_down, norm_mix, w_in,
              mix_rkv, mix_wag, w_lora_a, w_lora_b, w0, a_lora_a, a_lora_b, a0,
              g_lora_a, g_lora_b, k_k, k_a, r_k, ln_x_w, ln_x_b, proj_a, proj_b, w_out,
              norm_ffn2, ffn2_gate, ffn2_up, ffn2_down, norm_final):
    B, S, _ = x.shape
    splits = list(np.cumsum([W_MOBA, W_MOBA, W_MOBA, W_RWKV, W_RWKV, W_RWKV, D_MODEL])[:])
    for l in range(DEPTH):
        x = x + 0.5 * swiglu(rms_norm(x, norm_ffn1[l]), ffn1_gate[l], ffn1_up[l], ffn1_down[l])
        h = rms_norm(x, norm_mix[l])
        proj = h @ w_in[l]
        qa, ka, va, rb, kb, vb, ga, gb = jnp.split(proj, splits, axis=-1)
        to_heads = lambda z: z.reshape(B, S, N_HEADS_MOBA, HEAD_DIM)
        y_a = moba_attention(to_heads(qa), to_heads(ka), to_heads(va), rel_bias)
        y_b = rwkv7_time_mix(h, rb, kb, vb, mix_rkv[l], mix_wag[l], w_lora_a[l], w_lora_b[l],
                             w0[l], a_lora_a[l], a_lora_b[l], a0[l], g_lora_a[l],
                             g_lora_b[l], k_k[l], k_a[l], r_k[l], ln_x_w[l], ln_x_b[l])
        merged = jax.nn.sigmoid(ga) * (y_a @ proj_a[l]) + jax.nn.sigmoid(gb) * (y_b @ proj_b[l])
        x = x + merged @ w_out[l]
        x = x + 0.5 * swiglu(rms_norm(x, norm_ffn2[l]), ffn2_gate[l], ffn2_up[l], ffn2_down[l])
    return rms_norm(x, norm_final)
```

```python
import functools
import math

import numpy as np
import jax
import jax.numpy as jnp
from jax import lax
from jax.experimental import pallas as pl
from jax.experimental.pallas import tpu as pltpu

F32 = jnp.float32
BF16 = jnp.bfloat16

HEAD_DIM = 64
LANES = 128
MOBA_BLOCK = 256
MOBA_TOPK = 3
MAX_DISTANCE = 1024
RMS_EPS = 1e-6
GN_EPS = HEAD_DIM * 1e-5
RWKV_CHUNK = 64
NEG_BIG = -1e30
NEAR_TILES = 5
TOKEN_TILE = 512
FFN_CHUNK = 256
VMEM_LIMIT = 56 * 1024 * 1024

_NT = (((1,), (1,)), ((), ()))


def _dot(a, b):
    return jnp.dot(a, b, preferred_element_type=F32)


def _dot_nt(a, b):
    return lax.dot_general(a, b, _NT, preferred_element_type=F32)


def _rms(x, gain):
    ms = jnp.mean(x * x, axis=-1, keepdims=True)
    return x * lax.rsqrt(ms + RMS_EPS) * gain


def _resident(shape):
    nd = len(shape)
    return pl.BlockSpec(shape, lambda *_: (0,) * nd, pipeline_mode=pl.Buffered(1))


def _params(n_axes):
    return pltpu.CompilerParams(dimension_semantics=("arbitrary",) * n_axes, vmem_limit_bytes=VMEM_LIMIT)


def _ffn_kernel(x_ref, gain_ref, wgu_ref, wd_ref, fgain_ref, o_ref, act_ref, *, n_chunks, fc, final_norm):
    x = x_ref[...]
    h = _rms(x, gain_ref[...]).astype(BF16)
    for c in range(n_chunks):
        gu = _dot(h, wgu_ref[c])
        g = gu[:, :fc]
        u = gu[:, fc:]
        act_ref[:, c * fc:(c + 1) * fc] = (g * jax.nn.sigmoid(g) * u).astype(BF16)
    o = x + 0.5 * _dot(act_ref[...], wd_ref[...])
    if final_norm:
        o = _rms(o, fgain_ref[...])
    o_ref[...] = o


def _ffn(x2d, gain, w_gate, w_up, w_down, final_gain, *, final_norm):
    n, d = x2d.shape
    f = w_gate.shape[1]
    fc = FFN_CHUNK
    n_chunks = f // fc
    tm = TOKEN_TILE
    wg = w_gate.astype(BF16).reshape(d, n_chunks, fc).transpose(1, 0, 2)
    wu = w_up.astype(BF16).reshape(d, n_chunks, fc).transpose(1, 0, 2)
    wgu = jnp.concatenate([wg, wu], axis=-1)
    wd = w_down.astype(BF16)
    kern = functools.partial(_ffn_kernel, n_chunks=n_chunks, fc=fc, final_norm=final_norm)
    return pl.pallas_call(
        kern,
        grid=(n // tm,),
        in_specs=[
            pl.BlockSpec((tm, d), lambda i: (i, 0)),
            _resident((1, d)),
            _resident((n_chunks, d, 2 * fc)),
            _resident((f, d)),
            _resident((1, d)),
        ],
        out_specs=pl.BlockSpec((tm, d), lambda i: (i, 0)),
        out_shape=jax.ShapeDtypeStruct((n, d), F32),
        scratch_shapes=[pltpu.VMEM((tm, f), BF16)],
        compiler_params=_params(1),
        name="ffn_final" if final_norm else "ffn",
    )(x2d, gain.reshape(1, d), wgu, wd, final_gain.reshape(1, d))


def _shift_rows(z, carry_row):
    rolled = pltpu.roll(z, 1, 0)
    row = lax.broadcasted_iota(jnp.int32, z.shape, 0)
    return jnp.where(row == 0, carry_row, rolled)


def _inproj_kernel(x_ref, gain_ref, wq_ref, wk_ref, wvt_ref, wrkv_ref, wgate_ref, wla_ref, mixt_ref, mixrkv_ref,
                   q_ref, k_ref, vt_ref, kmean_ref, rkv_ref, gate_ref, lora_ref,
                   wl_scr, carry_rkv, carry_lora, *, tiles_per_seq, lora_pad, lora_cols):
    i = pl.program_id(0)
    tm = x_ref.shape[0]
    cw, ca, cg = lora_cols

    @pl.when(i == 0)
    def _():
        col = lax.broadcasted_iota(jnp.int32, (wla_ref.shape[0], lora_pad), 1)
        m = jnp.where(col < cw, mixt_ref[:, 0:1], jnp.where(col < cw + ca, mixt_ref[:, 1:2], mixt_ref[:, 2:3]))
        wla = wla_ref[...]
        wl_scr[:, :lora_pad] = ((1.0 - m) * wla).astype(BF16)
        wl_scr[:, lora_pad:] = (m * wla).astype(BF16)

    @pl.when(i % tiles_per_seq == 0)
    def _():
        carry_rkv[...] = jnp.zeros_like(carry_rkv)
        carry_lora[...] = jnp.zeros_like(carry_lora)

    h = _rms(x_ref[...], gain_ref[...]).astype(BF16)

    q_ref[...] = (_dot(h, wq_ref[...]) * (HEAD_DIM ** -0.5)).astype(BF16)
    kf = _dot(h, wk_ref[...])
    k_ref[...] = kf.astype(BF16)
    for j in range(tm // MOBA_BLOCK):
        kmean_ref[0, j:j + 1, :] = jnp.mean(kf[j * MOBA_BLOCK:(j + 1) * MOBA_BLOCK], axis=0, keepdims=True)
    vt = _dot_nt(wvt_ref[...], h)
    for j in range(tm // MOBA_BLOCK):
        vt_ref[0, j] = vt[:, j * MOBA_BLOCK:(j + 1) * MOBA_BLOCK].astype(BF16)

    rkv = _dot(h, wrkv_ref[...])
    prev = _shift_rows(rkv, carry_rkv[...])
    carry_rkv[...] = rkv[tm - 1:tm, :]
    rkv_ref[...] = rkv + (prev - rkv) * mixrkv_ref[...]

    gate_ref[...] = _dot(h, wgate_ref[...]).astype(BF16)

    lo = _dot(h, wl_scr[...])
    l1 = lo[:, :lora_pad]
    l2 = lo[:, lora_pad:]
    lora_ref[...] = l1 + _shift_rows(l2, carry_lora[...])
    carry_lora[...] = l2[tm - 1:tm, :]


def _inproj(x2d, gain, w_in, mix_rkv, mix_wag, w_lora_a, a_lora_a, g_lora_a, *, batch, seq):
    n, d = x2d.shape
    tm = TOKEN_TILE
    wm = 512
    nblk = tm // MOBA_BLOCK
    cw, ca, cg = w_lora_a.shape[1], a_lora_a.shape[1], g_lora_a.shape[1]
    lora_pad = -(-(cw + ca + cg) // LANES) * LANES
    wb = w_in.astype(BF16)
    wq, wk = wb[:, :wm], wb[:, wm:2 * wm]
    wvt = wb[:, 2 * wm:3 * wm].T
    wrkv = wb[:, 3 * wm:6 * wm]
    wgate = wb[:, 6 * wm:]
    wla = jnp.concatenate([w_lora_a, a_lora_a, g_lora_a, jnp.zeros((d, lora_pad - cw - ca - cg), F32)], axis=1)
    kern = functools.partial(_inproj_kernel, tiles_per_seq=seq // tm, lora_pad=lora_pad, lora_cols=(cw, ca, cg))
    row = lambda i: (i, 0)
    outs = pl.pallas_call(
        kern,
        grid=(n // tm,),
        in_specs=[
            pl.BlockSpec((tm, d), row),
            _resident((1, d)),
            _resident((d, wm)), _resident((d, wm)), _resident((wm, d)),
            _resident((d, 3 * wm)), _resident((d, 2 * d)),
            _resident((d, lora_pad)), _resident((d, 3)), _resident((1, 3 * wm)),
        ],
        out_specs=[
            pl.BlockSpec((tm, wm), row),
            pl.BlockSpec((tm, wm), row),
            pl.BlockSpec((1, nblk, wm, MOBA_BLOCK), lambda i: (i, 0, 0, 0)),
            pl.BlockSpec((1, nblk, wm), lambda i: (i, 0, 0)),
            pl.BlockSpec((tm, 3 * wm), row),
            pl.BlockSpec((tm, 2 * d), row),
            pl.BlockSpec((tm, lora_pad), row),
        ],
        out_shape=[
            jax.ShapeDtypeStruct((n, wm), BF16),
            jax.ShapeDtypeStruct((n, wm), BF16),
            jax.ShapeDtypeStruct((n // tm, nblk, wm, MOBA_BLOCK), BF16),
            jax.ShapeDtypeStruct((n // tm, nblk, wm), F32),
            jax.ShapeDtypeStruct((n, 3 * wm), F32),
            jax.ShapeDtypeStruct((n, 2 * d), BF16),
            jax.ShapeDtypeStruct((n, lora_pad), F32),
        ],
        scratch_shapes=[
            pltpu.VMEM((d, 2 * lora_pad), BF16),
            pltpu.VMEM((1, 3 * wm), F32),
            pltpu.VMEM((1, lora_pad), F32),
        ],
        compiler_params=_params(1),
        name="inproj",
    )(x2d, gain.reshape(1, d), wq, wk, wvt, wrkv, wgate, wla, mix_wag.T, mix_rkv.reshape(1, 3 * wm))
    return outs


def _t5_bucket_np(dist, num_buckets):
    n = np.maximum(dist, 0)
    max_exact = num_buckets // 2
    nf = np.maximum(n, max_exact).astype(np.float32)
    large = max_exact + (np.log(nf / np.float32(max_exact)) / np.float32(math.log(MAX_DISTANCE / max_exact))
                         * np.float32(num_buckets - max_exact)).astype(np.int32)
    large = np.minimum(large, num_buckets - 1)
    return np.where(n < max_exact, n, large).astype(np.int32)


def _bias_kernel(rb_ref, bkt_ref, o_ref, *, num_buckets):
    h = pl.program_id(0)
    bkt = bkt_ref[0]
    last = rb_ref[num_buckets - 1, h]
    acc = jnp.zeros(bkt.shape, F32)
    for b in range(num_buckets - 1):
        acc = jnp.where(bkt == b, rb_ref[b, h] - last, acc)
    o_ref[0, 0] = acc


def _bias_tiles(rel_bias, n_heads):
    num_buckets = rel_bias.shape[0]
    blk = MOBA_BLOCK
    key = np.arange(blk)[:, None]
    qry = np.arange(blk)[None, :]
    offs = np.arange(NEAR_TILES + 1)[:, None, None]
    bkt = _t5_bucket_np(offs * blk + qry - key, num_buckets)
    bkt[NEAR_TILES] = num_buckets - 1
    far = _t5_bucket_np(np.arange(NEAR_TILES * blk - blk + 1, 16 * blk * 64), num_buckets)
    assert (far == num_buckets - 1).all()
    return pl.pallas_call(
        functools.partial(_bias_kernel, num_buckets=num_buckets),
        grid=(n_heads, NEAR_TILES + 1),
        in_specs=[
            pl.BlockSpec(memory_space=pltpu.SMEM),
            pl.BlockSpec((1, blk, blk), lambda h, o: (o, 0, 0)),
        ],
        out_specs=pl.BlockSpec((1, 1, blk, blk), lambda h, o: (h, o, 0, 0)),
        out_shape=jax.ShapeDtypeStruct((n_heads, NEAR_TILES + 1, blk, blk), F32),
        compiler_params=_params(2),
        name="bias_tiles",
    )(rel_bias, jnp.asarray(bkt))


def _moba_kernel(q_ref, k_ref, vt_ref, kmean_ref, bias_ref, o_ref, msk_scr, *, topk):
    i = pl.program_id(2)
    blk = MOBA_BLOCK
    nb = k_ref.shape[0]
    hd = HEAD_DIM
    q2 = q_ref[...]
    lane = lax.broadcasted_iota(jnp.int32, q2.shape, 1)
    q_heads = [jnp.where((lane >= e * hd) & (lane < (e + 1) * hd), q2, jnp.zeros_like(q2)) for e in range(2)]

    km = kmean_ref[...]
    km_hi = km.astype(BF16)
    km_lo = (km - km_hi.astype(F32)).astype(BF16)
    nidx = lax.broadcasted_iota(jnp.int32, (nb, blk), 0)
    valid = nidx < i
    for e in range(2):
        gate = _dot_nt(km_hi, q_heads[e]) + _dot_nt(km_lo, q_heads[e])
        gv = jnp.where(valid, gate, -jnp.inf)
        cnt = jnp.zeros((nb, blk), jnp.int32)
        for n2 in range(nb):
            row = gv[n2:n2 + 1, :]
            beats = (row > gv) | ((row == gv) & (n2 < nidx))
            cnt = cnt + beats.astype(jnp.int32)
        sel = valid & (cnt < topk)
        msk_scr[e * nb:(e + 1) * nb, :] = jnp.where(sel, 0.0, NEG_BIG)

    key_i = lax.broadcasted_iota(jnp.int32, (blk, blk), 0)
    qry_i = lax.broadcasted_iota(jnp.int32, (blk, blk), 1)
    causal = key_i <= qry_i

    def tile(e, j, st, extra):
        m, l, acc = st
        s = _dot_nt(k_ref[j], q_heads[e]) + extra
        m_new = jnp.maximum(m, jnp.max(s, axis=0, keepdims=True))
        alpha = jnp.exp(m - m_new)
        p = jnp.exp(s - m_new)
        l = alpha * l + jnp.sum(p, axis=0, keepdims=True)
        acc = alpha * acc + _dot(vt_ref[0, j, e * hd:(e + 1) * hd, :], p.astype(BF16))
        return m_new, l, acc

    init = (jnp.full((1, blk), NEG_BIG, F32), jnp.zeros((1, blk), F32), jnp.zeros((hd, blk), F32))
    own = [tile(e, i, init, jnp.where(causal, bias_ref[e, 0], NEG_BIG)) for e in range(2)]

    def body(j, carry):
        t = jnp.minimum(i - j, NEAR_TILES)
        return tuple(tile(e, j, carry[e], bias_ref[e, t] + msk_scr[pl.ds(e * nb + j, 1), :]) for e in range(2))

    st = lax.fori_loop(0, i, body, tuple(own))
    out_t = jnp.concatenate([st[e][2] / st[e][1] for e in range(2)], axis=0)
    o_ref[...] = out_t.T.astype(BF16)


def _moba(q, k, vt, kmean, bias, *, batch, seq):
    n, wm = q.shape
    blk = MOBA_BLOCK
    nb = seq // blk
    n_pairs = wm // LANES
    topk = min(MOBA_TOPK, max(nb - 1, 1))
    k3 = k.reshape(n // blk, blk, wm)
    vt4 = vt.reshape(batch, nb, wm, blk)
    km2 = kmean.reshape(n // blk, wm)
    return pl.pallas_call(
        functools.partial(_moba_kernel, topk=topk),
        grid=(batch, n_pairs, nb),
        in_specs=[
            pl.BlockSpec((blk, LANES), lambda b, p, i: (b * nb + i, p)),
            pl.BlockSpec((nb, blk, LANES), lambda b, p, i: (b, 0, p)),
            pl.BlockSpec((1, nb, LANES, blk), lambda b, p, i: (b, 0, p, 0)),
            pl.BlockSpec((nb, LANES), lambda b, p, i: (b, p)),
            pl.BlockSpec((2, NEAR_TILES + 1, blk, blk), lambda b, p, i: (p, 0, 0, 0)),
        ],
        out_specs=pl.BlockSpec((blk, LANES), lambda b, p, i: (b * nb + i, p)),
        out_shape=jax.ShapeDtypeStruct((n, wm), BF16),
        scratch_shapes=[pltpu.VMEM((2 * nb, blk), F32)],
        compiler_params=_params(3),
        name="moba",
    )(q, k3, vt4, km2, bias)


def _head_sum(x, lo):
    zero = jnp.zeros_like(x)
    s0 = jnp.sum(jnp.where(lo, x, zero), axis=-1, keepdims=True)
    s1 = jnp.sum(jnp.where(lo, zero, x), axis=-1, keepdims=True)
    return jnp.where(lo, s0, s1)


def _split3(x):
    hi = x.astype(BF16)
    r1 = x - hi.astype(F32)
    mid = r1.astype(BF16)
    lo = (r1 - mid.astype(F32)).astype(BF16)
    return hi, mid, lo


def _rwkv_kernel(r_ref, k_ref, v_ref, lp_ref, wlb_ref, alb_ref, glb_ref, par_ref, o_ref, z_scr, y_scr):
    t = pl.program_id(2)
    ts = r_ref.shape[0]
    c = RWKV_CHUNK
    hd = HEAD_DIM

    @pl.when(t == 0)
    def _():
        z_scr[...] = jnp.zeros_like(z_scr)

    lane1 = lax.broadcasted_iota(jnp.int32, (1, LANES), 1)
    lo1 = lane1 < hd
    lane = lax.broadcasted_iota(jnp.int32, (ts, LANES), 1)
    lo = lane < hd

    par = par_ref[...]
    w0, a0, k_k, k_a, r_k, ln_w, ln_b = (par[j:j + 1, :] for j in range(7))

    lp = lp_ref[...]
    wa = lp[:, :LANES]
    wa_act = jnp.where(lo, jnp.tanh(wa), wa).astype(BF16)
    wpre = w0 + _dot(wa_act, wlb_ref[...])
    w_raw = jnp.minimum(wpre, 0.0) - jnp.log(1.0 + jnp.exp(-jnp.abs(wpre))) - 0.5
    lw = -jnp.exp(w_raw)
    iclr = jax.nn.sigmoid(a0 + _dot(wa_act, alb_ref[...]))
    g = _dot(jax.nn.sigmoid(lp[:, LANES:]).astype(BF16), glb_ref[...])

    r = r_ref[...]
    k = k_ref[...]
    v = v_ref[...]
    kk = k * k_k
    kk = kk / jnp.maximum(jnp.sqrt(_head_sum(kk * kk, lo)), 1e-12)
    k = k * (1.0 + (iclr - 1.0) * k_a)
    a_ = -kk
    b_ = kk * iclr
    bonus = _head_sum(r * k * r_k, lo) * v

    row_c = lax.broadcasted_iota(jnp.int32, (c, c), 0)
    col_c = lax.broadcasted_iota(jnp.int32, (c, c), 1)
    incl = row_c >= col_c
    strict = row_c > col_c
    tri = jnp.where(incl, 1.0, 0.0).astype(BF16)
    eye = jnp.where(row_c == col_c, 1.0, 0.0)
    row_p = lax.broadcasted_iota(jnp.int32, (LANES, LANES), 0)
    col_p = lax.broadcasted_iota(jnp.int32, (LANES, LANES), 1)
    same_head = (row_p < hd) == (col_p < hd)
    diag_p = row_p == col_p
    lo_c = lax.broadcasted_iota(jnp.int32, (c, LANES), 1) < hd
    lo_ar = lax.broadcasted_iota(jnp.int32, (2 * c, LANES), 1) < hd
    zeros_c = jnp.zeros((c, LANES), F32)

    z = z_scr[...]
    for ci in range(ts // c):
        sl = slice(ci * c, (ci + 1) * c)
        rc, kc, vc, ac, bc, lwc = r[sl], k[sl], v[sl], a_[sl], b_[sl], lw[sl]
        cum = sum(_dot(tri, part) for part in _split3(lwc))
        tot = cum[c - 1:c, :]
        e_neg = jnp.exp(-cum)
        e_fwd = jnp.exp(tot - cum)
        rt = rc * jnp.exp(cum)
        at = ac * jnp.exp(cum - lwc)
        bt = (bc * e_neg).astype(BF16)
        kt = (kc * e_neg).astype(BF16)
        bb = bc * e_fwd
        kb = kc * e_fwd
        p_c = jnp.exp(tot)
        vb = vc.astype(BF16)
        ar = jnp.concatenate([at, rt], axis=0)

        w_h, u_h, qw_h, y0_h = [], [], [], []
        for e in range(2):
            ar_e = jnp.where(lo_ar if e == 0 else ~lo_ar, ar, 0.0).astype(BF16)
            d1 = _dot_nt(ar_e, bt)
            d2 = _dot_nt(ar_e, kt)
            a_ab = jnp.where(strict, d1[:c], 0.0)
            a_rb = jnp.where(incl, d1[c:], 0.0).astype(BF16)
            a_ak = jnp.where(strict, d2[:c], 0.0).astype(BF16)
            a_rk = jnp.where(incl, d2[c:], 0.0).astype(BF16)
            tm = eye + a_ab
            pw = a_ab
            for _ in range(int(math.log2(c)) - 1):
                pwb = pw.astype(BF16)
                pw = _dot(pwb, pwb)
                tm = tm + _dot(tm.astype(BF16), pw.astype(BF16))
            akv = _dot(a_ak, vb)
            wu = _dot(tm.astype(BF16), jnp.concatenate([at, akv], axis=1).astype(BF16))
            qy = _dot(a_rb, wu.astype(BF16))
            w_h.append(wu[:, :LANES])
            u_h.append(wu[:, LANES:])
            qw_h.append(qy[:, :LANES])
            y0_h.append(qy[:, LANES:] + _dot(a_rk, vb))
        w2 = jnp.where(lo_c, w_h[0], w_h[1])
        u2 = jnp.where(lo_c, u_h[0], u_h[1])
        qe2 = rt + jnp.where(lo_c, qw_h[0], qw_h[1])
        y02 = jnp.where(lo_c, y0_h[0], y0_h[1])

        bkt = jnp.concatenate([bb, kb], axis=0).T.astype(BF16)
        rhs = jnp.concatenate([jnp.concatenate([w2, u2], axis=1),
                               jnp.concatenate([zeros_c, vc], axis=1)], axis=0).astype(BF16)
        mg = _dot(bkt, rhs)
        m2 = jnp.where(same_head, mg[:, :LANES], 0.0) + jnp.where(diag_p, p_c, 0.0)
        g2 = jnp.where(same_head, mg[:, LANES:], 0.0)
        zb = z.astype(BF16)
        y_scr[sl, :] = _dot(qe2.astype(BF16), zb) + y02
        z = _dot(m2.astype(BF16), zb) + g2
    z_scr[...] = z

    y = y_scr[...]
    mu = _head_sum(y, lo) * (1.0 / hd)
    dy = y - mu
    var = _head_sum(dy * dy, lo) * (1.0 / hd)
    yn = dy * lax.rsqrt(var + GN_EPS) * ln_w + ln_b
    o_ref[...] = ((yn + bonus) * g).astype(BF16)


def _rwkv(rkv, lora, w_lora_b, a_lora_b, g_lora_b, w0, a0, k_k, k_a, r_k, ln_x_w, ln_x_b, *, batch, seq):
    n = rkv.shape[0]
    wr = rkv.shape[1] // 3
    n_pairs = wr // LANES
    ts = TOKEN_TILE
    nt = seq // ts
    cw, ca, cg = w_lora_b.shape[0], a_lora_b.shape[0], g_lora_b.shape[0]
    lora_pad = lora.shape[1]
    assert cw == HEAD_DIM and ca == HEAD_DIM and cw + ca == LANES
    wlb = jnp.concatenate([w_lora_b, jnp.zeros((LANES - cw, wr), F32)], axis=0).astype(BF16)
    alb = jnp.concatenate([jnp.zeros((LANES - ca, wr), F32), a_lora_b], axis=0).astype(BF16)
    glb = jnp.concatenate([g_lora_b, jnp.zeros((lora_pad - LANES - cg, wr), F32)], axis=0).astype(BF16)
    par = jnp.stack([w0, a0, k_k, k_a, r_k.reshape(-1), ln_x_w, ln_x_b, jnp.zeros_like(w0)], axis=0)
    tok = lambda off: (lambda b, p, t: (b * nt + t, off + p))
    col = lambda b, p, t: (0, p)
    return pl.pallas_call(
        _rwkv_kernel,
        grid=(batch, n_pairs, nt),
        in_specs=[
            pl.BlockSpec((ts, LANES), tok(0)),
            pl.BlockSpec((ts, LANES), tok(n_pairs)),
            pl.BlockSpec((ts, LANES), tok(2 * n_pairs)),
            pl.BlockSpec((ts, lora_pad), lambda b, p, t: (b * nt + t, 0)),
            pl.BlockSpec((LANES, LANES), col),
            pl.BlockSpec((LANES, LANES), col),
            pl.BlockSpec((lora_pad - LANES, LANES), col),
            pl.BlockSpec((8, LANES), col),
        ],
        out_specs=pl.BlockSpec((ts, LANES), tok(0)),
        out_shape=jax.ShapeDtypeStruct((n, wr), BF16),
        scratch_shapes=[pltpu.VMEM((LANES, LANES), F32), pltpu.VMEM((ts, LANES), F32)],
        compiler_params=_params(3),
        name="rwkv7",
    )(rkv, rkv, rkv, lora, wlb, alb, glb, par)


def _merge_kernel(x_ref, ya_ref, yb_ref, gate_ref, pa_ref, pb_ref, wo_ref, o_ref):
    d = x_ref.shape[1]
    gate = gate_ref[...].astype(F32)
    merged = (jax.nn.sigmoid(gate[:, :d]) * _dot(ya_ref[...], pa_ref[...])
              + jax.nn.sigmoid(gate[:, d:]) * _dot(yb_ref[...], pb_ref[...]))
    o_ref[...] = x_ref[...] + _dot(merged.astype(BF16), wo_ref[...])


def _merge(x2d, ya, yb, gate, proj_a, proj_b, w_out):
    n, d = x2d.shape
    wm = ya.shape[1]
    tm = TOKEN_TILE
    row = lambda i: (i, 0)
    return pl.pallas_call(
        _merge_kernel,
        grid=(n // tm,),
        in_specs=[
            pl.BlockSpec((tm, d), row), pl.BlockSpec((tm, wm), row), pl.BlockSpec((tm, wm), row),
            pl.BlockSpec((tm, 2 * d), row),
            _resident((wm, d)), _resident((wm, d)), _resident((d, d)),
        ],
        out_specs=pl.BlockSpec((tm, d), row),
        out_shape=jax.ShapeDtypeStruct((n, d), F32),
        compiler_params=_params(1),
        name="merge",
    )(x2d, ya, yb, gate, proj_a.astype(BF16), proj_b.astype(BF16), w_out.astype(BF16))


def kernel(x, rel_bias, norm_ffn1, ffn1_gate, ffn1_up, ffn1_down, norm_mix, w_in, mix_rkv, mix_wag, w_lora_a, w_lora_b, w0, a_lora_a, a_lora_b, a0, g_lora_a, g_lora_b, k_k, k_a, r_k, ln_x_w, ln_x_b, proj_a, proj_b, w_out, norm_ffn2, ffn2_gate, ffn2_up, ffn2_down, norm_final):
    batch, seq, d = x.shape
    depth = norm_ffn1.shape[0]
    assert depth >= 1
    n_heads = rel_bias.shape[1]
    bias = _bias_tiles(rel_bias, n_heads)
    x2d = x.reshape(batch * seq, d)
    for l in range(depth):
        x2d = _ffn(x2d, norm_ffn1[l], ffn1_gate[l], ffn1_up[l], ffn1_down[l], norm_final, final_norm=False)
        q, k, vt, kmean, rkv, gate, lora = _inproj(
            x2d, norm_mix[l], w_in[l], mix_rkv[l], mix_wag[l], w_lora_a[l], a_lora_a[l], g_lora_a[l],
            batch=batch, seq=seq)
        ya = _moba(q, k, vt, kmean, bias, batch=batch, seq=seq)
        yb = _rwkv(rkv, lora, w_lora_b[l], a_lora_b[l], g_lora_b[l], w0[l], a0[l], k_k[l], k_a[l], r_k[l],
                   ln_x_w[l], ln_x_b[l], batch=batch, seq=seq)
        x2d = _merge(x2d, ya, yb, gate, proj_a[l], proj_b[l], w_out[l])
        x2d = _ffn(x2d, norm_ffn2[l], ffn2_gate[l], ffn2_up[l], ffn2_down[l], norm_final,
                   final_norm=(l == depth - 1))
    return x2d.reshape(batch, seq, d)
```

```python
import functools
import math

import numpy as np
import jax
import jax.numpy as jnp
from jax import lax
from jax.experimental import pallas as pl
from jax.experimental.pallas import tpu as pltpu

F32 = jnp.float32
BF16 = jnp.bfloat16

HEAD_DIM = 64
LANES = 128
MOBA_BLOCK = 256
MOBA_TOPK = 3
MOBA_GROUP = 4
MAX_DISTANCE = 1024
RMS_EPS = 1e-6
GN_EPS = HEAD_DIM * 1e-5
RWKV_CHUNK = 64
NEG_BIG = -1e30
NEAR_TILES = 5
TOKEN_TILE = 512
FFN_CHUNK = 256
VMEM_LIMIT = 56 * 1024 * 1024

_NT = (((1,), (1,)), ((), ()))


def _dot(a, b):
    return jnp.dot(a, b, preferred_element_type=F32)


def _dot_nt(a, b):
    return lax.dot_general(a, b, _NT, preferred_element_type=F32)


def _rms(x, gain):
    ms = jnp.mean(x * x, axis=-1, keepdims=True)
    return x * lax.rsqrt(ms + RMS_EPS) * gain


def _resident(shape):
    nd = len(shape)
    return pl.BlockSpec(shape, lambda *_: (0,) * nd, pipeline_mode=pl.Buffered(1))


def _params(n_axes):
    return pltpu.CompilerParams(dimension_semantics=("arbitrary",) * n_axes, vmem_limit_bytes=VMEM_LIMIT)


def _ffn_kernel(x_ref, gain_ref, wgu_ref, wd_ref, fgain_ref, o_ref, act_ref, *, n_chunks, fc, final_norm):
    x = x_ref[...]
    h = _rms(x, gain_ref[...]).astype(BF16)
    for c in range(n_chunks):
        gu = _dot(h, wgu_ref[c])
        g = gu[:, :fc]
        u = gu[:, fc:]
        act_ref[:, c * fc:(c + 1) * fc] = (g * jax.nn.sigmoid(g) * u).astype(BF16)
    o = x + 0.5 * _dot(act_ref[...], wd_ref[...])
    if final_norm:
        o = _rms(o, fgain_ref[...])
    o_ref[...] = o


def _ffn(x2d, gain, w_gate, w_up, w_down, final_gain, *, final_norm):
    n, d = x2d.shape
    f = w_gate.shape[1]
    fc = FFN_CHUNK
    n_chunks = f // fc
    tm = TOKEN_TILE
    wg = w_gate.astype(BF16).reshape(d, n_chunks, fc).transpose(1, 0, 2)
    wu = w_up.astype(BF16).reshape(d, n_chunks, fc).transpose(1, 0, 2)
    wgu = jnp.concatenate([wg, wu], axis=-1)
    wd = w_down.astype(BF16)
    kern = functools.partial(_ffn_kernel, n_chunks=n_chunks, fc=fc, final_norm=final_norm)
    return pl.pallas_call(
        kern,
        grid=(n // tm,),
        in_specs=[
            pl.BlockSpec((tm, d), lambda i: (i, 0)),
            _resident((1, d)),
            _resident((n_chunks, d, 2 * fc)),
            _resident((f, d)),
            _resident((1, d)),
        ],
        out_specs=pl.BlockSpec((tm, d), lambda i: (i, 0)),
        out_shape=jax.ShapeDtypeStruct((n, d), F32),
        scratch_shapes=[pltpu.VMEM((tm, f), BF16)],
        compiler_params=_params(1),
        name="ffn_final" if final_norm else "ffn",
    )(x2d, gain.reshape(1, d), wgu, wd, final_gain.reshape(1, d))


def _shift_rows(z, carry_row):
    rolled = pltpu.roll(z, 1, 0)
    row = lax.broadcasted_iota(jnp.int32, z.shape, 0)
    return jnp.where(row == 0, carry_row, rolled)


def _inproj_kernel(x_ref, gain_ref, wq_ref, wk_ref, wvt_ref, wrkv_ref, wgate_ref, wla_ref, mixt_ref, mixrkv_ref,
                   q_ref, k_ref, vt_ref, kmean_ref, rkv_ref, gate_ref, lora_ref,
                   wl_scr, carry_rkv, carry_lora, *, tiles_per_seq, lora_pad, lora_cols):
    i = pl.program_id(0)
    tm = x_ref.shape[0]
    cw, ca, cg = lora_cols

    @pl.when(i == 0)
    def _():
        col = lax.broadcasted_iota(jnp.int32, (wla_ref.shape[0], lora_pad), 1)
        m = jnp.where(col < cw, mixt_ref[:, 0:1], jnp.where(col < cw + ca, mixt_ref[:, 1:2], mixt_ref[:, 2:3]))
        wla = wla_ref[...]
        wl_scr[:, :lora_pad] = ((1.0 - m) * wla).astype(BF16)
        wl_scr[:, lora_pad:] = (m * wla).astype(BF16)

    @pl.when(i % tiles_per_seq == 0)
    def _():
        carry_rkv[...] = jnp.zeros_like(carry_rkv)
        carry_lora[...] = jnp.zeros_like(carry_lora)

    h = _rms(x_ref[...], gain_ref[...]).astype(BF16)

    q_ref[...] = (_dot(h, wq_ref[...]) * (HEAD_DIM ** -0.5)).astype(BF16)
    kf = _dot(h, wk_ref[...])
    k_ref[...] = kf.astype(BF16)
    for j in range(tm // MOBA_BLOCK):
        kmean_ref[0, j:j + 1, :] = jnp.mean(kf[j * MOBA_BLOCK:(j + 1) * MOBA_BLOCK], axis=0, keepdims=True)
    vt = _dot_nt(wvt_ref[...], h)
    for j in range(tm // MOBA_BLOCK):
        vt_ref[0, j] = vt[:, j * MOBA_BLOCK:(j + 1) * MOBA_BLOCK].astype(BF16)

    rkv = _dot(h, wrkv_ref[...])
    prev = _shift_rows(rkv, carry_rkv[...])
    carry_rkv[...] = rkv[tm - 1:tm, :]
    rkv_ref[...] = rkv + (prev - rkv) * mixrkv_ref[...]

    gate_ref[...] = _dot(h, wgate_ref[...]).astype(BF16)

    lo = _dot(h, wl_scr[...])
    l1 = lo[:, :lora_pad]
    l2 = lo[:, lora_pad:]
    lora_ref[...] = l1 + _shift_rows(l2, carry_lora[...])
    carry_lora[...] = l2[tm - 1:tm, :]


def _inproj(x2d, gain, w_in, mix_rkv, mix_wag, w_lora_a, a_lora_a, g_lora_a, *, batch, seq):
    n, d = x2d.shape
    tm = TOKEN_TILE
    wm = 512
    nblk = tm // MOBA_BLOCK
    cw, ca, cg = w_lora_a.shape[1], a_lora_a.shape[1], g_lora_a.shape[1]
    lora_pad = -(-(cw + ca + cg) // LANES) * LANES
    wb = w_in.astype(BF16)
    wq, wk = wb[:, :wm], wb[:, wm:2 * wm]
    wvt = wb[:, 2 * wm:3 * wm].T
    wrkv = wb[:, 3 * wm:6 * wm]
    wgate = wb[:, 6 * wm:]
    wla = jnp.concatenate([w_lora_a, a_lora_a, g_lora_a, jnp.zeros((d, lora_pad - cw - ca - cg), F32)], axis=1)
    kern = functools.partial(_inproj_kernel, tiles_per_seq=seq // tm, lora_pad=lora_pad, lora_cols=(cw, ca, cg))
    row = lambda i: (i, 0)
    outs = pl.pallas_call(
        kern,
        grid=(n // tm,),
        in_specs=[
            pl.BlockSpec((tm, d), row),
            _resident((1, d)),
            _resident((d, wm)), _resident((d, wm)), _resident((wm, d)),
            _resident((d, 3 * wm)), _resident((d, 2 * d)),
            _resident((d, lora_pad)), _resident((d, 3)), _resident((1, 3 * wm)),
        ],
        out_specs=[
            pl.BlockSpec((tm, wm), row),
            pl.BlockSpec((tm, wm), row),
            pl.BlockSpec((1, nblk, wm, MOBA_BLOCK), lambda i: (i, 0, 0, 0)),
            pl.BlockSpec((1, nblk, wm), lambda i: (i, 0, 0)),
            pl.BlockSpec((tm, 3 * wm), row),
            pl.BlockSpec((tm, 2 * d), row),
            pl.BlockSpec((tm, lora_pad), row),
        ],
        out_shape=[
            jax.ShapeDtypeStruct((n, wm), BF16),
            jax.ShapeDtypeStruct((n, wm), BF16),
            jax.ShapeDtypeStruct((n // tm, nblk, wm, MOBA_BLOCK), BF16),
            jax.ShapeDtypeStruct((n // tm, nblk, wm), F32),
            jax.ShapeDtypeStruct((n, 3 * wm), F32),
            jax.ShapeDtypeStruct((n, 2 * d), BF16),
            jax.ShapeDtypeStruct((n, lora_pad), F32),
        ],
        scratch_shapes=[
            pltpu.VMEM((d, 2 * lora_pad), BF16),
            pltpu.VMEM((1, 3 * wm), F32),
            pltpu.VMEM((1, lora_pad), F32),
        ],
        compiler_params=_params(1),
        name="inproj",
    )(x2d, gain.reshape(1, d), wq, wk, wvt, wrkv, wgate, wla, mix_wag.T, mix_rkv.reshape(1, 3 * wm))
    return outs


def _t5_bucket_np(dist, num_buckets):
    n = np.maximum(dist, 0)
    max_exact = num_buckets // 2
    nf = np.maximum(n, max_exact).astype(np.float32)
    large = max_exact + (np.log(nf / np.float32(max_exact)) / np.float32(math.log(MAX_DISTANCE / max_exact))
                         * np.float32(num_buckets - max_exact)).astype(np.int32)
    large = np.minimum(large, num_buckets - 1)
    return np.where(n < max_exact, n, large).astype(np.int32)


def _bias_kernel(rb_ref, bkt_ref, o_ref, *, num_buckets):
    h = pl.program_id(0)
    bkt = bkt_ref[0]
    last = rb_ref[num_buckets - 1, h]
    acc = jnp.zeros(bkt.shape, F32)
    for b in range(num_buckets - 1):
        acc = jnp.where(bkt == b, rb_ref[b, h] - last, acc)
    o_ref[0, 0] = acc


def _bias_tiles(rel_bias, n_heads):
    num_buckets = rel_bias.shape[0]
    blk = MOBA_BLOCK
    key = np.arange(blk)[:, None]
    qry = np.arange(blk)[None, :]
    offs = np.arange(NEAR_TILES + 1)[:, None, None]
    bkt = _t5_bucket_np(offs * blk + qry - key, num_buckets)
    bkt[NEAR_TILES] = num_buckets - 1
    far = _t5_bucket_np(np.arange(NEAR_TILES * blk - blk + 1, 16 * blk * 64), num_buckets)
    assert (far == num_buckets - 1).all()
    return pl.pallas_call(
        functools.partial(_bias_kernel, num_buckets=num_buckets),
        grid=(n_heads, NEAR_TILES + 1),
        in_specs=[
            pl.BlockSpec(memory_space=pltpu.SMEM),
            pl.BlockSpec((1, blk, blk), lambda h, o: (o, 0, 0)),
        ],
        out_specs=pl.BlockSpec((1, 1, blk, blk), lambda h, o: (h, o, 0, 0)),
        out_shape=jax.ShapeDtypeStruct((n_heads, NEAR_TILES + 1, blk, blk), F32),
        compiler_params=_params(2),
        name="bias_tiles",
    )(rel_bias, jnp.asarray(bkt))


def _moba_kernel(q_ref, k_ref, vt_ref, kmean_ref, bias_ref, o_ref, msk_scr, *, topk):
    i = pl.program_id(2)
    blk = MOBA_BLOCK
    nb = k_ref.shape[0]
    hd = HEAD_DIM
    q2 = q_ref[...]
    lane = lax.broadcasted_iota(jnp.int32, q2.shape, 1)
    q_heads = [jnp.where((lane >= e * hd) & (lane < (e + 1) * hd), q2, jnp.zeros_like(q2)) for e in range(2)]

    km = kmean_ref[...]
    km_hi = km.astype(BF16)
    km_lo = (km - km_hi.astype(F32)).astype(BF16)
    nidx = lax.broadcasted_iota(jnp.int32, (nb, blk), 0)
    valid = nidx < i
    for e in range(2):
        gate = _dot_nt(km_hi, q_heads[e]) + _dot_nt(km_lo, q_heads[e])
        gv = jnp.where(valid, gate, -jnp.inf)
        cnt = jnp.zeros((nb, blk), jnp.int32)
        for n2 in range(nb):
            row = gv[n2:n2 + 1, :]
            beats = (row > gv) | ((row == gv) & (n2 < nidx))
            cnt = cnt + beats.astype(jnp.int32)
        sel = valid & (cnt < topk)
        msk_scr[e * nb:(e + 1) * nb, :] = jnp.where(sel, 0.0, NEG_BIG)

    key_i = lax.broadcasted_iota(jnp.int32, (blk, blk), 0)
    qry_i = lax.broadcasted_iota(jnp.int32, (blk, blk), 1)
    causal = key_i <= qry_i

    def update(st, s_tiles, vt_tiles):
        m, l, acc = st
        m_new = m
        for s in s_tiles:
            m_new = jnp.maximum(m_new, jnp.max(s, axis=0, keepdims=True))
        alpha = jnp.exp(m - m_new)
        l = alpha * l
        acc = alpha * acc
        for s, vt_t in zip(s_tiles, vt_tiles):
            p = jnp.exp(s - m_new)
            l = l + jnp.sum(p, axis=0, keepdims=True)
            acc = acc + _dot(vt_t, p.astype(BF16))
        return m_new, l, acc

    init = (jnp.full((1, blk), NEG_BIG, F32), jnp.zeros((1, blk), F32), jnp.zeros((hd, blk), F32))
    s_own = [_dot_nt(k_ref[i], q_heads[e]) for e in range(2)]
    own = tuple(update(init, [jnp.where(causal, s_own[e] + bias_ref[e, 0], NEG_BIG)],
                       [vt_ref[0, i, e * hd:(e + 1) * hd, :]]) for e in range(2))

    grp = MOBA_GROUP

    def group_body(g, carry, near):
        j0 = g * grp
        kg = k_ref[pl.ds(j0, grp)].reshape(grp * blk, LANES)
        s_all = [_dot_nt(kg, q_heads[e]) for e in range(2)]
        new = []
        for e in range(2):
            s_tiles = []
            for jj in range(grp):
                s = s_all[e][jj * blk:(jj + 1) * blk] + msk_scr[pl.ds(e * nb + j0 + jj, 1), :]
                if near:
                    s = s + bias_ref[e, jnp.clip(i - (j0 + jj), 0, NEAR_TILES)]
                s_tiles.append(s)
            vt_tiles = [vt_ref[0, j0 + jj, e * hd:(e + 1) * hd, :] for jj in range(grp)]
            new.append(update(carry[e], s_tiles, vt_tiles))
        return tuple(new)

    n_groups = (i + grp - 1) // grp
    n_far = jnp.maximum(i - NEAR_TILES + 1, 0) // grp
    st = lax.fori_loop(0, n_far, functools.partial(group_body, near=False), own)
    st = lax.fori_loop(n_far, n_groups, functools.partial(group_body, near=True), st)
    out_t = jnp.concatenate([st[e][2] / st[e][1] for e in range(2)], axis=0)
    o_ref[...] = out_t.T.astype(BF16)


def _moba(q, k, vt, kmean, bias, *, batch, seq):
    n, wm = q.shape
    blk = MOBA_BLOCK
    nb = seq // blk
    n_pairs = wm // LANES
    topk = min(MOBA_TOPK, max(nb - 1, 1))
    k3 = k.reshape(n // blk, blk, wm)
    vt4 = vt.reshape(batch, nb, wm, blk)
    km2 = kmean.reshape(n // blk, wm)
    return pl.pallas_call(
        functools.partial(_moba_kernel, topk=topk),
        grid=(batch, n_pairs, nb),
        in_specs=[
            pl.BlockSpec((blk, LANES), lambda b, p, i: (b * nb + i, p)),
            pl.BlockSpec((nb, blk, LANES), lambda b, p, i: (b, 0, p)),
            pl.BlockSpec((1, nb, LANES, blk), lambda b, p, i: (b, 0, p, 0)),
            pl.BlockSpec((nb, LANES), lambda b, p, i: (b, p)),
            pl.BlockSpec((2, NEAR_TILES + 1, blk, blk), lambda b, p, i: (p, 0, 0, 0)),
        ],
        out_specs=pl.BlockSpec((blk, LANES), lambda b, p, i: (b * nb + i, p)),
        out_shape=jax.ShapeDtypeStruct((n, wm), BF16),
        scratch_shapes=[pltpu.VMEM((2 * nb, blk), F32)],
        compiler_params=_params(3),
        name="moba",
    )(q, k3, vt4, km2, bias)


def _head_sum(x, lo):
    zero = jnp.zeros_like(x)
    s0 = jnp.sum(jnp.where(lo, x, zero), axis=-1, keepdims=True)
    s1 = jnp.sum(jnp.where(lo, zero, x), axis=-1, keepdims=True)
    return jnp.where(lo, s0, s1)


def _rwkv_kernel(r_ref, k_ref, v_ref, lp_ref, wlb_ref, alb_ref, glb_ref, par_ref, o_ref, z_scr, y_scr):
    t = pl.program_id(2)
    ts = r_ref.shape[0]
    c = RWKV_CHUNK
    hd = HEAD_DIM

    @pl.when(t == 0)
    def _():
        z_scr[...] = jnp.zeros_like(z_scr)

    lane = lax.broadcasted_iota(jnp.int32, (ts, LANES), 1)
    lo = lane < hd

    par = par_ref[...]
    w0, a0, k_k, k_a, r_k, ln_w, ln_b = (par[j:j + 1, :] for j in range(7))

    lp = lp_ref[...]
    wa = lp[:, :LANES]
    wa_act = jnp.where(lo, jnp.tanh(wa), wa).astype(BF16)
    wpre = w0 + _dot(wa_act, wlb_ref[...])
    w_raw = jnp.minimum(wpre, 0.0) - jnp.log(1.0 + jnp.exp(-jnp.abs(wpre))) - 0.5
    lw = -jnp.exp(w_raw)
    iclr = jax.nn.sigmoid(a0 + _dot(wa_act, alb_ref[...]))
    g = _dot(jax.nn.sigmoid(lp[:, LANES:]).astype(BF16), glb_ref[...])

    r = r_ref[...]
    k = k_ref[...]
    v = v_ref[...]
    kk = k * k_k
    kk = kk / jnp.maximum(jnp.sqrt(_head_sum(kk * kk, lo)), 1e-12)
    k = k * (1.0 + (iclr - 1.0) * k_a)
    a_ = -kk
    b_ = kk * iclr
    bonus = _head_sum(r * k * r_k, lo) * v

    row_c = lax.broadcasted_iota(jnp.int32, (c, c), 0)
    col_c = lax.broadcasted_iota(jnp.int32, (c, c), 1)
    tri = jnp.where(row_c >= col_c, 1.0, 0.0).astype(BF16)
    eye = jnp.where(row_c == col_c, 1.0, 0.0)
    row_p = lax.broadcasted_iota(jnp.int32, (LANES, LANES), 0)
    col_p = lax.broadcasted_iota(jnp.int32, (LANES, LANES), 1)
    same_head = (row_p < hd) == (col_p < hd)
    diag_p = row_p == col_p
    lo_ar = lax.broadcasted_iota(jnp.int32, (2 * c, LANES), 1) < hd

    nc = ts // c
    units = [(ci, e) for ci in range(nc) for e in range(2)]
    row_w = lax.broadcasted_iota(jnp.int32, (c, LANES), 0)
    col_w = lax.broadcasted_iota(jnp.int32, (c, LANES), 1) & (c - 1)
    strict2 = row_w > col_w
    incl2 = row_w >= col_w
    zeros_cb = jnp.zeros((c, LANES), BF16)

    lw_hi = lw.astype(BF16)
    lw_mid = (lw - lw_hi.astype(F32)).astype(BF16)
    cums = [_dot(tri, lw_hi[ci * c:(ci + 1) * c]) + _dot(tri, lw_mid[ci * c:(ci + 1) * c])
            for ci in range(nc)]
    ch = []
    for ci in range(nc):
        sl = slice(ci * c, (ci + 1) * c)
        rc, kc, vc, ac, bc, lwc, cum = r[sl], k[sl], v[sl], a_[sl], b_[sl], lw[sl], cums[ci]
        tot = cum[c - 1:c, :]
        e_neg = jnp.exp(-cum)
        e_fwd = jnp.exp(tot - cum)
        rt = rc * jnp.exp(cum)
        at = ac * jnp.exp(cum - lwc)
        vb = vc.astype(BF16)
        ch.append(dict(
            rt=rt, at=at, vc=vc, p_c=jnp.exp(tot),
            ar=jnp.concatenate([at, rt], axis=0),
            btkt=jnp.concatenate([bc * e_neg, kc * e_neg], axis=0).astype(BF16),
            bbkb=jnp.concatenate([bc * e_fwd, kc * e_fwd], axis=0),
            vpad=jnp.concatenate([zeros_cb, vb], axis=0),
            zv=jnp.concatenate([zeros_cb, vb], axis=1),
        ))

    dtop, dbot = {}, {}
    for (ci, e) in units:
        ar_e = jnp.where(lo_ar if e == 0 else ~lo_ar, ch[ci]["ar"], 0.0).astype(BF16)
        d = _dot_nt(ar_e, ch[ci]["btkt"])
        dtop[ci, e] = jnp.where(strict2, d[:c], 0.0)
        dbot[ci, e] = jnp.where(incl2, d[c:], 0.0).astype(BF16)

    pw = {u: dtop[u][:, :c] for u in units}
    tm = {u: eye + pw[u] for u in units}
    for u in units:
        pb = pw[u].astype(BF16)
        pw[u] = _dot(pb, pb)
    for _ in range(int(math.log2(c)) - 2):
        for u in units:
            pb = pw[u].astype(BF16)
            sq = _dot(jnp.concatenate([pb, tm[u].astype(BF16)], axis=0), pb)
            pw[u] = sq[:c]
            tm[u] = tm[u] + sq[c:]
    for u in units:
        tm[u] = tm[u] + _dot(tm[u].astype(BF16), pw[u].astype(BF16))

    akv = {(ci, e): _dot(dtop[ci, e].astype(BF16), ch[ci]["vpad"]) for (ci, e) in units}
    wu = {(ci, e): _dot(tm[ci, e].astype(BF16),
                        jnp.concatenate([ch[ci]["at"], akv[ci, e]], axis=1).astype(BF16))
          for (ci, e) in units}
    qy = {(ci, e): _dot(dbot[ci, e], jnp.concatenate([wu[ci, e].astype(BF16), ch[ci]["zv"]], axis=0))
          for (ci, e) in units}

    m2s, g2s, qes, y0s = [], [], [], []
    lo_c2 = (lax.broadcasted_iota(jnp.int32, (c, 2 * LANES), 1) & (LANES - 1)) < hd
    for ci in range(nc):
        wu2 = jnp.where(lo_c2, wu[ci, 0], wu[ci, 1])
        qy2 = jnp.where(lo_c2, qy[ci, 0], qy[ci, 1])
        bkt = ch[ci]["bbkb"].T.astype(BF16)
        rhs = jnp.concatenate([wu2.astype(BF16), ch[ci]["zv"]], axis=0)
        mg = _dot(bkt, rhs)
        m2s.append((jnp.where(same_head, mg[:, :LANES], 0.0) + jnp.where(diag_p, ch[ci]["p_c"], 0.0)).astype(BF16))
        g2s.append(jnp.where(same_head, mg[:, LANES:], 0.0))
        qes.append((ch[ci]["rt"] + qy2[:, :LANES]).astype(BF16))
        y0s.append(qy2[:, LANES:])

    z = z_scr[...]
    for ci in range(nc):
        zb = z.astype(BF16)
        y_scr[ci * c:(ci + 1) * c, :] = _dot(qes[ci], zb) + y0s[ci]
        z = _dot(m2s[ci], zb) + g2s[ci]
    z_scr[...] = z

    y = y_scr[...]
    mu = _head_sum(y, lo) * (1.0 / hd)
    dy = y - mu
    var = _head_sum(dy * dy, lo) * (1.0 / hd)
    yn = dy * lax.rsqrt(var + GN_EPS) * ln_w + ln_b
    o_ref[...] = ((yn + bonus) * g).astype(BF16)


def _rwkv(rkv, lora, w_lora_b, a_lora_b, g_lora_b, w0, a0, k_k, k_a, r_k, ln_x_w, ln_x_b, *, batch, seq):
    n = rkv.shape[0]
    wr = rkv.shape[1] // 3
    n_pairs = wr // LANES
    ts = TOKEN_TILE
    nt = seq // ts
    cw, ca, cg = w_lora_b.shape[0], a_lora_b.shape[0], g_lora_b.shape[0]
    lora_pad = lora.shape[1]
    assert cw == HEAD_DIM and ca == HEAD_DIM and cw + ca == LANES
    wlb = jnp.concatenate([w_lora_b, jnp.zeros((LANES - cw, wr), F32)], axis=0).astype(BF16)
    alb = jnp.concatenate([jnp.zeros((LANES - ca, wr), F32), a_lora_b], axis=0).astype(BF16)
    glb = jnp.concatenate([g_lora_b, jnp.zeros((lora_pad - LANES - cg, wr), F32)], axis=0).astype(BF16)
    par = jnp.stack([w0, a0, k_k, k_a, r_k.reshape(-1), ln_x_w, ln_x_b, jnp.zeros_like(w0)], axis=0)
    tok = lambda off: (lambda b, p, t: (b * nt + t, off + p))
    col = lambda b, p, t: (0, p)
    return pl.pallas_call(
        _rwkv_kernel,
        grid=(batch, n_pairs, nt),
        in_specs=[
            pl.BlockSpec((ts, LANES), tok(0)),
            pl.BlockSpec((ts, LANES), tok(n_pairs)),
            pl.BlockSpec((ts, LANES), tok(2 * n_pairs)),
            pl.BlockSpec((ts, lora_pad), lambda b, p, t: (b * nt + t, 0)),
            pl.BlockSpec((LANES, LANES), col),
            pl.BlockSpec((LANES, LANES), col),
            pl.BlockSpec((lora_pad - LANES, LANES), col),
            pl.BlockSpec((8, LANES), col),
        ],
        out_specs=pl.BlockSpec((ts, LANES), tok(0)),
        out_shape=jax.ShapeDtypeStruct((n, wr), BF16),
        scratch_shapes=[pltpu.VMEM((LANES, LANES), F32), pltpu.VMEM((ts, LANES), F32)],
        compiler_params=_params(3),
        name="rwkv7",
    )(rkv, rkv, rkv, lora, wlb, alb, glb, par)


def _merge_kernel(x_ref, ya_ref, yb_ref, gate_ref, pa_ref, pb_ref, wo_ref, o_ref):
    d = x_ref.shape[1]
    gate = gate_ref[...].astype(F32)
    merged = (jax.nn.sigmoid(gate[:, :d]) * _dot(ya_ref[...], pa_ref[...])
              + jax.nn.sigmoid(gate[:, d:]) * _dot(yb_ref[...], pb_ref[...]))
    o_ref[...] = x_ref[...] + _dot(merged.astype(BF16), wo_ref[...])


def _merge(x2d, ya, yb, gate, proj_a, proj_b, w_out):
    n, d = x2d.shape
    wm = ya.shape[1]
    tm = TOKEN_TILE
    row = lambda i: (i, 0)
    return pl.pallas_call(
        _merge_kernel,
        grid=(n // tm,),
        in_specs=[
            pl.BlockSpec((tm, d), row), pl.BlockSpec((tm, wm), row), pl.BlockSpec((tm, wm), row),
            pl.BlockSpec((tm, 2 * d), row),
            _resident((wm, d)), _resident((wm, d)), _resident((d, d)),
        ],
        out_specs=pl.BlockSpec((tm, d), row),
        out_shape=jax.ShapeDtypeStruct((n, d), F32),
        compiler_params=_params(1),
        name="merge",
    )(x2d, ya, yb, gate, proj_a.astype(BF16), proj_b.astype(BF16), w_out.astype(BF16))


def kernel(x, rel_bias, norm_ffn1, ffn1_gate, ffn1_up, ffn1_down, norm_mix, w_in, mix_rkv, mix_wag, w_lora_a, w_lora_b, w0, a_lora_a, a_lora_b, a0, g_lora_a, g_lora_b, k_k, k_a, r_k, ln_x_w, ln_x_b, proj_a, proj_b, w_out, norm_ffn2, ffn2_gate, ffn2_up, ffn2_down, norm_final):
    batch, seq, d = x.shape
    depth = norm_ffn1.shape[0]
    assert depth >= 1
    n_heads = rel_bias.shape[1]
    bias = _bias_tiles(rel_bias, n_heads)
    x2d = x.reshape(batch * seq, d)
    for l in range(depth):
        x2d = _ffn(x2d, norm_ffn1[l], ffn1_gate[l], ffn1_up[l], ffn1_down[l], norm_final, final_norm=False)
        q, k, vt, kmean, rkv, gate, lora = _inproj(
            x2d, norm_mix[l], w_in[l], mix_rkv[l], mix_wag[l], w_lora_a[l], a_lora_a[l], g_lora_a[l],
            batch=batch, seq=seq)
        ya = _moba(q, k, vt, kmean, bias, batch=batch, seq=seq)
        yb = _rwkv(rkv, lora, w_lora_b[l], a_lora_b[l], g_lora_b[l], w0[l], a0[l], k_k[l], k_a[l], r_k[l],
                   ln_x_w[l], ln_x_b[l], batch=batch, seq=seq)
        x2d = _merge(x2d, ya, yb, gate, proj_a[l], proj_b[l], w_out[l])
        x2d = _ffn(x2d, norm_ffn2[l], ffn2_gate[l], ffn2_up[l], ffn2_down[l], norm_final,
                   final_norm=(l == depth - 1))
    return x2d.reshape(batch, seq, d)
```

```python
import functools
import math

import numpy as np
import jax
import jax.numpy as jnp
from jax import lax
from jax.experimental import pallas as pl
from jax.experimental.pallas import tpu as pltpu

F32 = jnp.float32
BF16 = jnp.bfloat16

HEAD_DIM = 64
LANES = 128
MOBA_BLOCK = 256
MOBA_TOPK = 3
MOBA_GROUP = 4
MOBA_ONES_ROWS = 16
MAX_DISTANCE = 1024
RMS_EPS = 1e-6
GN_EPS = HEAD_DIM * 1e-5
RWKV_CHUNK = 64
NEG_BIG = -1e30
LOG2E = math.log2(math.e)
NEAR_TILES = 5
TOKEN_TILE = 512
FFN_CHUNK = 256
VMEM_LIMIT = 56 * 1024 * 1024

_NT = (((1,), (1,)), ((), ()))


def _dot(a, b):
    return jnp.dot(a, b, preferred_element_type=F32)


def _dot_nt(a, b):
    return lax.dot_general(a, b, _NT, preferred_element_type=F32)


def _rms(x, gain):
    ms = jnp.mean(x * x, axis=-1, keepdims=True)
    return x * lax.rsqrt(ms + RMS_EPS) * gain


def _resident(shape):
    nd = len(shape)
    return pl.BlockSpec(shape, lambda *_: (0,) * nd, pipeline_mode=pl.Buffered(1))


def _params(n_axes):
    return pltpu.CompilerParams(dimension_semantics=("arbitrary",) * n_axes, vmem_limit_bytes=VMEM_LIMIT)


def _ffn_kernel(x_ref, gain_ref, wgu_ref, wd_ref, fgain_ref, o_ref, act_ref, *, n_chunks, fc, final_norm):
    x = x_ref[...]
    h = _rms(x, gain_ref[...]).astype(BF16)
    for c in range(n_chunks):
        gu = _dot(h, wgu_ref[c])
        g = gu[:, :fc]
        u = gu[:, fc:]
        act_ref[:, c * fc:(c + 1) * fc] = (g * jax.nn.sigmoid(g) * u).astype(BF16)
    o = x + 0.5 * _dot(act_ref[...], wd_ref[...])
    if final_norm:
        o = _rms(o, fgain_ref[...])
    o_ref[...] = o


def _ffn(x2d, gain, w_gate, w_up, w_down, final_gain, *, final_norm):
    n, d = x2d.shape
    f = w_gate.shape[1]
    fc = FFN_CHUNK
    n_chunks = f // fc
    tm = TOKEN_TILE
    wg = w_gate.astype(BF16).reshape(d, n_chunks, fc).transpose(1, 0, 2)
    wu = w_up.astype(BF16).reshape(d, n_chunks, fc).transpose(1, 0, 2)
    wgu = jnp.concatenate([wg, wu], axis=-1)
    wd = w_down.astype(BF16)
    kern = functools.partial(_ffn_kernel, n_chunks=n_chunks, fc=fc, final_norm=final_norm)
    return pl.pallas_call(
        kern,
        grid=(n // tm,),
        in_specs=[
            pl.BlockSpec((tm, d), lambda i: (i, 0)),
            _resident((1, d)),
            _resident((n_chunks, d, 2 * fc)),
            _resident((f, d)),
            _resident((1, d)),
        ],
        out_specs=pl.BlockSpec((tm, d), lambda i: (i, 0)),
        out_shape=jax.ShapeDtypeStruct((n, d), F32),
        scratch_shapes=[pltpu.VMEM((tm, f), BF16)],
        compiler_params=_params(1),
        name="ffn_final" if final_norm else "ffn",
    )(x2d, gain.reshape(1, d), wgu, wd, final_gain.reshape(1, d))


def _shift_rows(z, carry_row):
    rolled = pltpu.roll(z, 1, 0)
    row = lax.broadcasted_iota(jnp.int32, z.shape, 0)
    return jnp.where(row == 0, carry_row, rolled)


def _inproj_kernel(x_ref, gain_ref, wqt_ref, wk_ref, wvt_ref, wrkv_ref, wgate_ref, wla_ref, mixt_ref, mixrkv_ref,
                   qt_ref, k_ref, vt_ref, kmean_ref, rkv_ref, gate_ref, lora_ref,
                   wl_scr, carry_rkv, carry_lora, *, tiles_per_seq, lora_pad, lora_cols):
    i = pl.program_id(0)
    tm = x_ref.shape[0]
    cw, ca, cg = lora_cols

    @pl.when(i == 0)
    def _():
        col = lax.broadcasted_iota(jnp.int32, (wla_ref.shape[0], lora_pad), 1)
        m = jnp.where(col < cw, mixt_ref[:, 0:1], jnp.where(col < cw + ca, mixt_ref[:, 1:2], mixt_ref[:, 2:3]))
        wla = wla_ref[...]
        wl_scr[:, :lora_pad] = ((1.0 - m) * wla).astype(BF16)
        wl_scr[:, lora_pad:] = (m * wla).astype(BF16)

    @pl.when(i % tiles_per_seq == 0)
    def _():
        carry_rkv[...] = jnp.zeros_like(carry_rkv)
        carry_lora[...] = jnp.zeros_like(carry_lora)

    h = _rms(x_ref[...], gain_ref[...]).astype(BF16)

    qt = _dot_nt(wqt_ref[...], h) * (HEAD_DIM ** -0.5 * LOG2E)
    for j in range(tm // MOBA_BLOCK):
        qt_ref[0, j] = qt[:, j * MOBA_BLOCK:(j + 1) * MOBA_BLOCK].astype(BF16)
    kf = _dot(h, wk_ref[...])
    k_ref[...] = kf.astype(BF16)
    for j in range(tm // MOBA_BLOCK):
        kmean_ref[0, j:j + 1, :] = jnp.mean(kf[j * MOBA_BLOCK:(j + 1) * MOBA_BLOCK], axis=0, keepdims=True)
    vt = _dot_nt(wvt_ref[...], h)
    for j in range(tm // MOBA_BLOCK):
        vt_ref[0, j] = vt[:, j * MOBA_BLOCK:(j + 1) * MOBA_BLOCK].astype(BF16)

    rkv = _dot(h, wrkv_ref[...])
    prev = _shift_rows(rkv, carry_rkv[...])
    carry_rkv[...] = rkv[tm - 1:tm, :]
    rkv_ref[...] = rkv + (prev - rkv) * mixrkv_ref[...]

    gate_ref[...] = _dot(h, wgate_ref[...]).astype(BF16)

    lo = _dot(h, wl_scr[...])
    l1 = lo[:, :lora_pad]
    l2 = lo[:, lora_pad:]
    lora_ref[...] = l1 + _shift_rows(l2, carry_lora[...])
    carry_lora[...] = l2[tm - 1:tm, :]


def _inproj(x2d, gain, w_in, mix_rkv, mix_wag, w_lora_a, a_lora_a, g_lora_a, *, batch, seq):
    n, d = x2d.shape
    tm = TOKEN_TILE
    wm = 512
    nblk = tm // MOBA_BLOCK
    cw, ca, cg = w_lora_a.shape[1], a_lora_a.shape[1], g_lora_a.shape[1]
    lora_pad = -(-(cw + ca + cg) // LANES) * LANES
    wb = w_in.astype(BF16)
    wqt, wk = wb[:, :wm].T, wb[:, wm:2 * wm]
    wvt = wb[:, 2 * wm:3 * wm].T
    wrkv = wb[:, 3 * wm:6 * wm]
    wgate = wb[:, 6 * wm:]
    wla = jnp.concatenate([w_lora_a, a_lora_a, g_lora_a, jnp.zeros((d, lora_pad - cw - ca - cg), F32)], axis=1)
    kern = functools.partial(_inproj_kernel, tiles_per_seq=seq // tm, lora_pad=lora_pad, lora_cols=(cw, ca, cg))
    row = lambda i: (i, 0)
    outs = pl.pallas_call(
        kern,
        grid=(n // tm,),
        in_specs=[
            pl.BlockSpec((tm, d), row),
            _resident((1, d)),
            _resident((wm, d)), _resident((d, wm)), _resident((wm, d)),
            _resident((d, 3 * wm)), _resident((d, 2 * d)),
            _resident((d, lora_pad)), _resident((d, 3)), _resident((1, 3 * wm)),
        ],
        out_specs=[
            pl.BlockSpec((1, nblk, wm, MOBA_BLOCK), lambda i: (i, 0, 0, 0)),
            pl.BlockSpec((tm, wm), row),
            pl.BlockSpec((1, nblk, wm, MOBA_BLOCK), lambda i: (i, 0, 0, 0)),
            pl.BlockSpec((1, nblk, wm), lambda i: (i, 0, 0)),
            pl.BlockSpec((tm, 3 * wm), row),
            pl.BlockSpec((tm, 2 * d), row),
            pl.BlockSpec((tm, lora_pad), row),
        ],
        out_shape=[
            jax.ShapeDtypeStruct((n // tm, nblk, wm, MOBA_BLOCK), BF16),
            jax.ShapeDtypeStruct((n, wm), BF16),
            jax.ShapeDtypeStruct((n // tm, nblk, wm, MOBA_BLOCK), BF16),
            jax.ShapeDtypeStruct((n // tm, nblk, wm), F32),
            jax.ShapeDtypeStruct((n, 3 * wm), F32),
            jax.ShapeDtypeStruct((n, 2 * d), BF16),
            jax.ShapeDtypeStruct((n, lora_pad), F32),
        ],
        scratch_shapes=[
            pltpu.VMEM((d, 2 * lora_pad), BF16),
            pltpu.VMEM((1, 3 * wm), F32),
            pltpu.VMEM((1, lora_pad), F32),
        ],
        compiler_params=_params(1),
        name="inproj",
    )(x2d, gain.reshape(1, d), wqt, wk, wvt, wrkv, wgate, wla, mix_wag.T, mix_rkv.reshape(1, 3 * wm))
    return outs


def _t5_bucket_np(dist, num_buckets):
    n = np.maximum(dist, 0)
    max_exact = num_buckets // 2
    nf = np.maximum(n, max_exact).astype(np.float32)
    large = max_exact + (np.log(nf / np.float32(max_exact)) / np.float32(math.log(MAX_DISTANCE / max_exact))
                         * np.float32(num_buckets - max_exact)).astype(np.int32)
    large = np.minimum(large, num_buckets - 1)
    return np.where(n < max_exact, n, large).astype(np.int32)


def _bias_kernel(rb_ref, bkt_ref, o_ref, *, num_buckets, bucket_ranges):
    h = pl.program_id(0)
    last = rb_ref[num_buckets - 1, h]
    blk = bkt_ref.shape[1]
    causal = lax.broadcasted_iota(jnp.int32, (blk, blk), 0) <= lax.broadcasted_iota(jnp.int32, (blk, blk), 1)
    for o, (b_lo, b_hi) in enumerate(bucket_ranges):
        bkt = bkt_ref[o]
        acc = jnp.zeros(bkt.shape, F32)
        for b in range(b_lo, min(b_hi, num_buckets - 2) + 1):
            acc = jnp.where(bkt == b, rb_ref[b, h] - last, acc)
        acc = acc * LOG2E
        o_ref[0, o] = jnp.where(causal, acc, NEG_BIG) if o == 0 else acc


def _bias_tiles(rel_bias, n_heads):
    num_buckets = rel_bias.shape[0]
    blk = MOBA_BLOCK
    key = np.arange(blk)[:, None]
    qry = np.arange(blk)[None, :]
    offs = np.arange(NEAR_TILES)[:, None, None]
    bkt = _t5_bucket_np(offs * blk + qry - key, num_buckets)
    far = _t5_bucket_np(np.arange(NEAR_TILES * blk - blk + 1, 16 * blk * 64), num_buckets)
    assert (far == num_buckets - 1).all()
    ranges = tuple((int(bkt[o].min()), int(bkt[o].max())) for o in range(NEAR_TILES))
    return pl.pallas_call(
        functools.partial(_bias_kernel, num_buckets=num_buckets, bucket_ranges=ranges),
        grid=(n_heads,),
        in_specs=[
            pl.BlockSpec(memory_space=pltpu.SMEM),
            _resident((NEAR_TILES, blk, blk)),
        ],
        out_specs=pl.BlockSpec((1, NEAR_TILES, blk, blk), lambda h: (h, 0, 0, 0)),
        out_shape=jax.ShapeDtypeStruct((n_heads, NEAR_TILES, blk, blk), F32),
        compiler_params=_params(1),
        name="bias_tiles",
    )(rel_bias, jnp.asarray(bkt))


def _moba_kernel(qa_ref, qb_ref, k_ref, vt_ref, kmean_ref, bias_ref, hot_ref, o_ref, *, topk):
    p = pl.program_id(2)
    blk = MOBA_BLOCK
    nb = k_ref.shape[0]
    hd = HEAD_DIM
    grp = MOBA_GROUP
    i_a = p
    i_b = nb - 1 - p

    chan = lax.broadcasted_iota(jnp.int32, (LANES, blk), 0)
    head_rows = [chan < hd, chan >= hd]
    km = kmean_ref[...]
    km_hi = km.astype(BF16)
    km_lo = (km - km_hi.astype(F32)).astype(BF16)
    nidx = lax.broadcasted_iota(jnp.int32, (nb, blk), 0)
    ones_rows = jnp.ones((MOBA_ONES_ROWS, blk), BF16)
    dead_rows = jnp.full((LANES - nb, blk), NEG_BIG, BF16)

    def query_operands(qt_ref, i):
        qt2 = qt_ref[0, 0]
        out = []
        for e in range(2):
            qt_e = jnp.where(head_rows[e], qt2, jnp.zeros_like(qt2))
            gate = _dot(km_hi, qt_e) + _dot(km_lo, qt_e)
            valid = nidx < i
            gv = jnp.where(valid, gate, -jnp.inf)
            cnt = jnp.zeros((nb, blk), jnp.int32)
            for n2 in range(nb):
                row = gv[n2:n2 + 1, :]
                beats = (row > gv) | ((row == gv) & (n2 < nidx))
                cnt = cnt + beats.astype(jnp.int32)
            sel = valid & (cnt < topk)
            msk = jnp.where(sel, 0.0, NEG_BIG).astype(BF16)
            out.append((qt_e, jnp.concatenate([qt_e, msk, dead_rows], axis=0)))
        return out

    def scores(tile):
        who, qops, j, bias_idx = tile
        if bias_idx == 0:
            s = [_dot(k_ref[j], qops[e][0]) for e in range(2)]
        else:
            k_aug = jnp.concatenate([k_ref[jnp.maximum(j, 0)], hot_ref[jnp.where(j >= 0, j, nb)]], axis=1)
            s = [_dot(k_aug, qops[e][1]) for e in range(2)]
        if bias_idx is not None:
            s = [s[e] + bias_ref[e, bias_idx] for e in range(2)]
        return s

    def update(st, e, s, j):
        m, acc = st
        m_new = jnp.maximum(m, jnp.max(s, axis=0, keepdims=True))
        vt_t = jnp.concatenate([vt_ref[0, jnp.maximum(j, 0), e * hd:(e + 1) * hd, :], ones_rows], axis=0)
        return m_new, jnp.exp2(m - m_new) * acc + _dot(vt_t, jnp.exp2(s - m_new).astype(BF16))

    def past_tiles(who, qops, i, g, near):
        return [(who, qops, i - grp * (g + 1) + t, (grp - t) if near else None) for t in range(grp)]

    qa = query_operands(qa_ref, i_a)
    qb = query_operands(qb_ref, i_b)
    a_turn = i_a > grp
    qx = [(None, jnp.where(a_turn, qa[e][1], qb[e][1])) for e in range(2)]
    tiles = ([("b", qb, i_b, 0), ("a", qa, i_a, 0)]
             + past_tiles("b", qb, i_b, 0, True) + past_tiles("b", qb, i_b, 1, False)
             + past_tiles("a", qa, i_a, 0, True) + past_tiles("b", qb, i_b, 2, False)
             + past_tiles("x", qx, jnp.where(a_turn, i_a, i_b), jnp.where(a_turn, 1, 3), False))

    init = (jnp.full((1, blk), NEG_BIG, F32), jnp.zeros((hd + MOBA_ONES_ROWS, blk), F32))
    st = {"a": [init, init], "b": [init, init]}
    ahead = 2
    pending = [scores(t) for t in tiles[:ahead]]
    for n, tile in enumerate(tiles):
        if n + ahead < len(tiles):
            pending.append(scores(tiles[n + ahead]))
        s = pending.pop(0)
        who, _, j, _ = tile
        if who == "x" and "x" not in st:
            st["x"] = [tuple(jnp.where(a_turn, ua, ub) for ua, ub in zip(st["a"][e], st["b"][e])) for e in range(2)]
        st[who] = [update(st[who][e], e, s[e], j) for e in range(2)]
    st_a = [tuple(jnp.where(a_turn, ux, ua) for ux, ua in zip(st["x"][e], st["a"][e])) for e in range(2)]
    st_b = [tuple(jnp.where(a_turn, ub, ux) for ux, ub in zip(st["x"][e], st["b"][e])) for e in range(2)]

    for half, sth in enumerate((st_a, st_b)):
        out_t = jnp.concatenate([sth[e][1][:hd] / sth[e][1][hd:hd + 1] for e in range(2)], axis=0)
        o_ref[half * blk:(half + 1) * blk, :] = out_t.T.astype(BF16)


def _moba(qt, k, vt, kmean, bias, *, batch, seq):
    n, wm = k.shape
    blk = MOBA_BLOCK
    nb = seq // blk
    n_pairs = wm // LANES
    grp = MOBA_GROUP
    topk = min(MOBA_TOPK, max(nb - 1, 1))
    assert nb == 4 * grp and NEAR_TILES == grp + 1 and nb + 1 <= LANES
    k3 = k.reshape(n // blk, blk, wm)
    qt4 = qt.reshape(batch, nb, wm, blk)
    vt4 = vt.reshape(batch, nb, wm, blk)
    km2 = kmean.reshape(n // blk, wm)
    hot = np.zeros((nb + 1, blk, LANES), np.float32)
    hot[np.arange(nb + 1), :, np.arange(nb + 1)] = 1.0
    half = nb // 2
    return pl.pallas_call(
        functools.partial(_moba_kernel, topk=topk),
        grid=(batch, n_pairs, half),
        in_specs=[
            pl.BlockSpec((1, 1, LANES, blk), lambda b, hp, p: (b, p, hp, 0)),
            pl.BlockSpec((1, 1, LANES, blk), lambda b, hp, p: (b, nb - 1 - p, hp, 0)),
            pl.BlockSpec((nb, blk, LANES), lambda b, hp, p: (b, 0, hp)),
            pl.BlockSpec((1, nb, LANES, blk), lambda b, hp, p: (b, 0, hp, 0)),
            pl.BlockSpec((nb, LANES), lambda b, hp, p: (b, hp)),
            pl.BlockSpec((2, NEAR_TILES, blk, blk), lambda b, hp, p: (hp, 0, 0, 0)),
            _resident((nb + 1, blk, LANES)),
        ],
        out_specs=pl.BlockSpec((2 * blk, LANES), lambda b, hp, p: (b * half + p, hp)),
        out_shape=jax.ShapeDtypeStruct((n, wm), BF16),
        compiler_params=_params(3),
        name="moba",
    )(qt4, qt4, k3, vt4, km2, bias, jnp.asarray(hot, BF16))


def _head_sum(x, lo):
    zero = jnp.zeros_like(x)
    s0 = jnp.sum(jnp.where(lo, x, zero), axis=-1, keepdims=True)
    s1 = jnp.sum(jnp.where(lo, zero, x), axis=-1, keepdims=True)
    return jnp.where(lo, s0, s1)


def _rwkv_kernel(r_ref, k_ref, v_ref, lp_ref, wlb_ref, alb_ref, glb_ref, par_ref, o_ref, z_scr, y_scr):
    t = pl.program_id(2)
    ts = r_ref.shape[0]
    c = RWKV_CHUNK
    hd = HEAD_DIM

    @pl.when(t == 0)
    def _():
        z_scr[...] = jnp.zeros_like(z_scr)

    lane = lax.broadcasted_iota(jnp.int32, (ts, LANES), 1)
    lo = lane < hd

    par = par_ref[...]
    w0, a0, k_k, k_a, r_k, ln_w, ln_b = (par[j:j + 1, :] for j in range(7))

    lp = lp_ref[...]
    wa = lp[:, :LANES]
    wa_act = jnp.where(lo, jnp.tanh(wa), wa).astype(BF16)
    wpre = w0 + _dot(wa_act, wlb_ref[...])
    w_raw = jnp.minimum(wpre, 0.0) - jnp.log(1.0 + jnp.exp(-jnp.abs(wpre))) - 0.5
    lw = -jnp.exp(w_raw)
    iclr = jax.nn.sigmoid(a0 + _dot(wa_act, alb_ref[...]))
    g = _dot(jax.nn.sigmoid(lp[:, LANES:]).astype(BF16), glb_ref[...])

    r = r_ref[...]
    k = k_ref[...]
    v = v_ref[...]
    kk = k * k_k
    kk = kk / jnp.maximum(jnp.sqrt(_head_sum(kk * kk, lo)), 1e-12)
    k = k * (1.0 + (iclr - 1.0) * k_a)
    a_ = -kk
    b_ = kk * iclr
    bonus = _head_sum(r * k * r_k, lo) * v

    row_c = lax.broadcasted_iota(jnp.int32, (c, c), 0)
    col_c = lax.broadcasted_iota(jnp.int32, (c, c), 1)
    tri = jnp.where(row_c >= col_c, 1.0, 0.0).astype(BF16)
    eye = jnp.where(row_c == col_c, 1.0, 0.0)
    row_p = lax.broadcasted_iota(jnp.int32, (LANES, LANES), 0)
    col_p = lax.broadcasted_iota(jnp.int32, (LANES, LANES), 1)
    same_head = (row_p < hd) == (col_p < hd)
    diag_p = row_p == col_p
    lo_ar = lax.broadcasted_iota(jnp.int32, (2 * c, LANES), 1) < hd

    nc = ts // c
    units = [(ci, e) for ci in range(nc) for e in range(2)]
    row_w = lax.broadcasted_iota(jnp.int32, (c, LANES), 0)
    col_w = lax.broadcasted_iota(jnp.int32, (c, LANES), 1) & (c - 1)
    strict2 = row_w > col_w
    incl2 = row_w >= col_w
    zeros_cb = jnp.zeros((c, LANES), BF16)

    lw_hi = lw.astype(BF16)
    lw_mid = (lw - lw_hi.astype(F32)).astype(BF16)
    cums = [_dot(tri, lw_hi[ci * c:(ci + 1) * c]) + _dot(tri, lw_mid[ci * c:(ci + 1) * c])
            for ci in range(nc)]
    ch = []
    for ci in range(nc):
        sl = slice(ci * c, (ci + 1) * c)
        rc, kc, vc, ac, bc, lwc, cum = r[sl], k[sl], v[sl], a_[sl], b_[sl], lw[sl], cums[ci]
        tot = cum[c - 1:c, :]
        e_neg = jnp.exp(-cum)
        e_fwd = jnp.exp(tot - cum)
        rt = rc * jnp.exp(cum)
        at = ac * jnp.exp(cum - lwc)
        vb = vc.astype(BF16)
        ch.append(dict(
            rt=rt, at=at, vc=vc, p_c=jnp.exp(tot),
            ar=jnp.concatenate([at, rt], axis=0),
            btkt=jnp.concatenate([bc * e_neg, kc * e_neg], axis=0).astype(BF16),
            bbkb=jnp.concatenate([bc * e_fwd, kc * e_fwd], axis=0),
            vpad=jnp.concatenate([zeros_cb, vb], axis=0),
            zv=jnp.concatenate([zeros_cb, vb], axis=1),
        ))

    dtop, dbot = {}, {}
    for (ci, e) in units:
        ar_e = jnp.where(lo_ar if e == 0 else ~lo_ar, ch[ci]["ar"], 0.0).astype(BF16)
        d = _dot_nt(ar_e, ch[ci]["btkt"])
        dtop[ci, e] = jnp.where(strict2, d[:c], 0.0)
        dbot[ci, e] = jnp.where(incl2, d[c:], 0.0).astype(BF16)

    pw = {u: dtop[u][:, :c] for u in units}
    tm = {u: eye + pw[u] for u in units}
    for u in units:
        pb = pw[u].astype(BF16)
        pw[u] = _dot(pb, pb)
    for _ in range(int(math.log2(c)) - 2):
        for u in units:
            pb = pw[u].astype(BF16)
            sq = _dot(jnp.concatenate([pb, tm[u].astype(BF16)], axis=0), pb)
            pw[u] = sq[:c]
            tm[u] = tm[u] + sq[c:]
    for u in units:
        tm[u] = tm[u] + _dot(tm[u].astype(BF16), pw[u].astype(BF16))

    akv = {(ci, e): _dot(dtop[ci, e].astype(BF16), ch[ci]["vpad"]) for (ci, e) in units}
    wu = {(ci, e): _dot(tm[ci, e].astype(BF16),
                        jnp.concatenate([ch[ci]["at"], akv[ci, e]], axis=1).astype(BF16))
          for (ci, e) in units}
    qy = {(ci, e): _dot(dbot[ci, e], jnp.concatenate([wu[ci, e].astype(BF16), ch[ci]["zv"]], axis=0))
          for (ci, e) in units}

    m2s, g2s, qes, y0s = [], [], [], []
    lo_c2 = (lax.broadcasted_iota(jnp.int32, (c, 2 * LANES), 1) & (LANES - 1)) < hd
    for ci in range(nc):
        wu2 = jnp.where(lo_c2, wu[ci, 0], wu[ci, 1])
        qy2 = jnp.where(lo_c2, qy[ci, 0], qy[ci, 1])
        bkt = ch[ci]["bbkb"].T.astype(BF16)
        rhs = jnp.concatenate([wu2.astype(BF16), ch[ci]["zv"]], axis=0)
        mg = _dot(bkt, rhs)
        m2s.append((jnp.where(same_head, mg[:, :LANES], 0.0) + jnp.where(diag_p, ch[ci]["p_c"], 0.0)).astype(BF16))
        g2s.append(jnp.where(same_head, mg[:, LANES:], 0.0))
        qes.append((ch[ci]["rt"] + qy2[:, :LANES]).astype(BF16))
        y0s.append(qy2[:, LANES:])

    z = z_scr[...]
    for ci in range(nc):
        zb = z.astype(BF16)
        y_scr[ci * c:(ci + 1) * c, :] = _dot(qes[ci], zb) + y0s[ci]
        z = _dot(m2s[ci], zb) + g2s[ci]
    z_scr[...] = z

    y = y_scr[...]
    mu = _head_sum(y, lo) * (1.0 / hd)
    dy = y - mu
    var = _head_sum(dy * dy, lo) * (1.0 / hd)
    yn = dy * lax.rsqrt(var + GN_EPS) * ln_w + ln_b
    o_ref[...] = ((yn + bonus) * g).astype(BF16)


def _rwkv(rkv, lora, w_lora_b, a_lora_b, g_lora_b, w0, a0, k_k, k_a, r_k, ln_x_w, ln_x_b, *, batch, seq):
    n = rkv.shape[0]
    wr = rkv.shape[1] // 3
    n_pairs = wr // LANES
    ts = TOKEN_TILE
    nt = seq // ts
    cw, ca, cg = w_lora_b.shape[0], a_lora_b.shape[0], g_lora_b.shape[0]
    lora_pad = lora.shape[1]
    assert cw == HEAD_DIM and ca == HEAD_DIM and cw + ca == LANES
    wlb = jnp.concatenate([w_lora_b, jnp.zeros((LANES - cw, wr), F32)], axis=0).astype(BF16)
    alb = jnp.concatenate([jnp.zeros((LANES - ca, wr), F32), a_lora_b], axis=0).astype(BF16)
    glb = jnp.concatenate([g_lora_b, jnp.zeros((lora_pad - LANES - cg, wr), F32)], axis=0).astype(BF16)
    par = jnp.stack([w0, a0, k_k, k_a, r_k.reshape(-1), ln_x_w, ln_x_b, jnp.zeros_like(w0)], axis=0)
    tok = lambda off: (lambda b, p, t: (b * nt + t, off + p))
    col = lambda b, p, t: (0, p)
    return pl.pallas_call(
        _rwkv_kernel,
        grid=(batch, n_pairs, nt),
        in_specs=[
            pl.BlockSpec((ts, LANES), tok(0)),
            pl.BlockSpec((ts, LANES), tok(n_pairs)),
            pl.BlockSpec((ts, LANES), tok(2 * n_pairs)),
            pl.BlockSpec((ts, lora_pad), lambda b, p, t: (b * nt + t, 0)),
            pl.BlockSpec((LANES, LANES), col),
            pl.BlockSpec((LANES, LANES), col),
            pl.BlockSpec((lora_pad - LANES, LANES), col),
            pl.BlockSpec((8, LANES), col),
        ],
        out_specs=pl.BlockSpec((ts, LANES), tok(0)),
        out_shape=jax.ShapeDtypeStruct((n, wr), BF16),
        scratch_shapes=[pltpu.VMEM((LANES, LANES), F32), pltpu.VMEM((ts, LANES), F32)],
        compiler_params=_params(3),
        name="rwkv7",
    )(rkv, rkv, rkv, lora, wlb, alb, glb, par)


def _merge_kernel(x_ref, ya0_ref, ya1_ref, yb_ref, gate_ref, pa_ref, pb_ref, wo_ref, o_ref):
    d = x_ref.shape[1]
    gate = gate_ref[...].astype(F32)
    ya = jnp.concatenate([ya0_ref[...], ya1_ref[...]], axis=0)
    merged = (jax.nn.sigmoid(gate[:, :d]) * _dot(ya, pa_ref[...])
              + jax.nn.sigmoid(gate[:, d:]) * _dot(yb_ref[...], pb_ref[...]))
    o_ref[...] = x_ref[...] + _dot(merged.astype(BF16), wo_ref[...])


def _merge(x2d, ya, yb, gate, proj_a, proj_b, w_out, *, seq):
    n, d = x2d.shape
    wm = ya.shape[1]
    tm = 2 * MOBA_BLOCK
    nb = seq // MOBA_BLOCK
    tiles_per_seq = seq // tm
    row = lambda i: (i, 0)

    def ya_block(which):
        def index(i):
            blk_i = 2 * (i % tiles_per_seq) + which
            pos = jnp.where(blk_i < nb // 2, 2 * blk_i, 2 * (nb - 1 - blk_i) + 1)
            return ((i // tiles_per_seq) * nb + pos, 0)
        return pl.BlockSpec((MOBA_BLOCK, wm), index)

    return pl.pallas_call(
        _merge_kernel,
        grid=(n // tm,),
        in_specs=[
            pl.BlockSpec((tm, d), row), ya_block(0), ya_block(1), pl.BlockSpec((tm, wm), row),
            pl.BlockSpec((tm, 2 * d), row),
            _resident((wm, d)), _resident((wm, d)), _resident((d, d)),
        ],
        out_specs=pl.BlockSpec((tm, d), row),
        out_shape=jax.ShapeDtypeStruct((n, d), F32),
        compiler_params=_params(1),
        name="merge",
    )(x2d, ya, ya, yb, gate, proj_a.astype(BF16), proj_b.astype(BF16), w_out.astype(BF16))


def kernel(x, rel_bias, norm_ffn1, ffn1_gate, ffn1_up, ffn1_down, norm_mix, w_in, mix_rkv, mix_wag, w_lora_a, w_lora_b, w0, a_lora_a, a_lora_b, a0, g_lora_a, g_lora_b, k_k, k_a, r_k, ln_x_w, ln_x_b, proj_a, proj_b, w_out, norm_ffn2, ffn2_gate, ffn2_up, ffn2_down, norm_final):
    batch, seq, d = x.shape
    depth = norm_ffn1.shape[0]
    assert depth >= 1
    n_heads = rel_bias.shape[1]
    bias = _bias_tiles(rel_bias, n_heads)
    x2d = x.reshape(batch * seq, d)
    for l in range(depth):
        x2d = _ffn(x2d, norm_ffn1[l], ffn1_gate[l], ffn1_up[l], ffn1_down[l], norm_final, final_norm=False)
        qt, k, vt, kmean, rkv, gate, lora = _inproj(
            x2d, norm_mix[l], w_in[l], mix_rkv[l], mix_wag[l], w_lora_a[l], a_lora_a[l], g_lora_a[l],
            batch=batch, seq=seq)
        ya = _moba(qt, k, vt, kmean, bias, batch=batch, seq=seq)
        yb = _rwkv(rkv, lora, w_lora_b[l], a_lora_b[l], g_lora_b[l], w0[l], a0[l], k_k[l], k_a[l], r_k[l],
                   ln_x_w[l], ln_x_b[l], batch=batch, seq=seq)
        x2d = _merge(x2d, ya, yb, gate, proj_a[l], proj_b[l], w_out[l], seq=seq)
        x2d = _ffn(x2d, norm_ffn2[l], ffn2_gate[l], ffn2_up[l], ffn2_down[l], norm_final,
                   final_norm=(l == depth - 1))
    return x2d.reshape(batch, seq, d)
```

```python
import functools
import math

import numpy as np
import jax
import jax.numpy as jnp
from jax import lax
from jax.experimental import pallas as pl
from jax.experimental.pallas import tpu as pltpu

F32 = jnp.float32
BF16 = jnp.bfloat16

HEAD_DIM = 64
LANES = 128
MOBA_BLOCK = 256
MOBA_TOPK = 3
MOBA_GROUP = 4
MOBA_ONES_ROWS = 16
MAX_DISTANCE = 1024
RMS_EPS = 1e-6
GN_EPS = HEAD_DIM * 1e-5
RWKV_CHUNK = 64
RWKV_PAIRS = 4
NEG_BIG = -1e30
LOG2E = math.log2(math.e)
NEAR_TILES = 5
TOKEN_TILE = 512
FFN_CHUNK = 256
VMEM_LIMIT = 56 * 1024 * 1024

_NT = (((1,), (1,)), ((), ()))


def _dot(a, b):
    return jnp.dot(a, b, preferred_element_type=F32)


def _dot_nt(a, b):
    return lax.dot_general(a, b, _NT, preferred_element_type=F32)


def _rms(x, gain):
    ms = jnp.mean(x * x, axis=-1, keepdims=True)
    return x * lax.rsqrt(ms + RMS_EPS) * gain


def _resident(shape):
    nd = len(shape)
    return pl.BlockSpec(shape, lambda *_: (0,) * nd, pipeline_mode=pl.Buffered(1))


def _params(n_axes):
    return pltpu.CompilerParams(dimension_semantics=("arbitrary",) * n_axes, vmem_limit_bytes=VMEM_LIMIT)


def _ffn_kernel(x_ref, gain_ref, wg_ref, wu_ref, wd_ref, fgain_ref, o_ref, act_ref, *, n_chunks, fc, final_norm):
    x = x_ref[...]
    h = _rms(x, gain_ref[...]).astype(BF16)
    for c in range(n_chunks):
        g = _dot(h, wg_ref[:, c * fc:(c + 1) * fc])
        u = _dot(h, wu_ref[:, c * fc:(c + 1) * fc])
        act_ref[:, c * fc:(c + 1) * fc] = (g * jax.nn.sigmoid(g) * u).astype(BF16)
    o = x + 0.5 * _dot(act_ref[...], wd_ref[...])
    if final_norm:
        o = _rms(o, fgain_ref[...])
    o_ref[...] = o


def _ffn(x2d, gain, w_gate, w_up, w_down, final_gain, *, final_norm):
    n, d = x2d.shape
    f = w_gate.shape[1]
    fc = FFN_CHUNK
    n_chunks = f // fc
    tm = TOKEN_TILE
    kern = functools.partial(_ffn_kernel, n_chunks=n_chunks, fc=fc, final_norm=final_norm)
    return pl.pallas_call(
        kern,
        grid=(n // tm,),
        in_specs=[
            pl.BlockSpec((tm, d), lambda i: (i, 0)),
            _resident((1, d)),
            _resident((d, f)),
            _resident((d, f)),
            _resident((f, d)),
            _resident((1, d)),
        ],
        out_specs=pl.BlockSpec((tm, d), lambda i: (i, 0)),
        out_shape=jax.ShapeDtypeStruct((n, d), F32),
        scratch_shapes=[pltpu.VMEM((tm, f), BF16)],
        compiler_params=_params(1),
        name="ffn_final" if final_norm else "ffn",
    )(x2d, gain.reshape(1, d), w_gate.astype(BF16), w_up.astype(BF16), w_down.astype(BF16),
      final_gain.reshape(1, d))


def _shift_rows(z, carry_row):
    rolled = pltpu.roll(z, 1, 0)
    row = lax.broadcasted_iota(jnp.int32, z.shape, 0)
    return jnp.where(row == 0, carry_row, rolled)


def _inproj_kernel(x_ref, gain_ref, wqvt_ref, w_ref, wla_ref, mixt_ref, mixrkv_ref,
                   qt_ref, k_ref, vt_ref, kmean_ref, rkv_ref, gate_ref, lora_ref,
                   wl_scr, carry_rkv, carry_lora, *, tiles_per_seq, lora_pad, lora_cols):
    i = pl.program_id(0)
    tm = x_ref.shape[0]
    wm = k_ref.shape[1]
    cw, ca, cg = lora_cols

    @pl.when(i == 0)
    def _():
        col = lax.broadcasted_iota(jnp.int32, (wla_ref.shape[0], lora_pad), 1)
        m = jnp.where(col < cw, mixt_ref[:, 0:1], jnp.where(col < cw + ca, mixt_ref[:, 1:2], mixt_ref[:, 2:3]))
        wla = wla_ref[...]
        wl_scr[:, :lora_pad] = ((1.0 - m) * wla).astype(BF16)
        wl_scr[:, lora_pad:] = (m * wla).astype(BF16)

    @pl.when(i % tiles_per_seq == 0)
    def _():
        carry_rkv[...] = jnp.zeros_like(carry_rkv)
        carry_lora[...] = jnp.zeros_like(carry_lora)

    h = _rms(x_ref[...], gain_ref[...]).astype(BF16)

    qvt = _dot_nt(wqvt_ref[...], h)
    for j in range(tm // MOBA_BLOCK):
        cols = slice(j * MOBA_BLOCK, (j + 1) * MOBA_BLOCK)
        qt_ref[0, j] = (qvt[:wm, cols] * (HEAD_DIM ** -0.5 * LOG2E)).astype(BF16)
        vt_ref[0, j] = qvt[wm:, cols].astype(BF16)
    kf = _dot(h, w_ref[:, wm:2 * wm])
    k_ref[...] = kf.astype(BF16)
    for j in range(tm // MOBA_BLOCK):
        kmean_ref[0, j:j + 1, :] = jnp.mean(kf[j * MOBA_BLOCK:(j + 1) * MOBA_BLOCK], axis=0, keepdims=True)

    rkv = _dot(h, w_ref[:, 3 * wm:6 * wm])
    prev = _shift_rows(rkv, carry_rkv[...])
    carry_rkv[...] = rkv[tm - 1:tm, :]
    rkv_ref[...] = rkv + (prev - rkv) * mixrkv_ref[...]

    gate_ref[...] = _dot(h, w_ref[:, 6 * wm:]).astype(BF16)

    lo = _dot(h, wl_scr[...])
    l1 = lo[:, :lora_pad]
    l2 = lo[:, lora_pad:]
    lora_ref[...] = l1 + _shift_rows(l2, carry_lora[...])
    carry_lora[...] = l2[tm - 1:tm, :]


def _inproj(x2d, gain, w_in, mix_rkv, mix_wag, w_lora_a, a_lora_a, g_lora_a, *, batch, seq):
    n, d = x2d.shape
    tm = TOKEN_TILE
    wm = (w_in.shape[1] - 2 * d) // 6
    nblk = tm // MOBA_BLOCK
    cw, ca, cg = w_lora_a.shape[1], a_lora_a.shape[1], g_lora_a.shape[1]
    lora_pad = -(-(cw + ca + cg) // LANES) * LANES
    wb = w_in.astype(BF16)
    wqvt = jnp.concatenate([wb[:, :wm], wb[:, 2 * wm:3 * wm]], axis=1).T
    wla = jnp.concatenate([w_lora_a, a_lora_a, g_lora_a, jnp.zeros((d, lora_pad - cw - ca - cg), F32)], axis=1)
    kern = functools.partial(_inproj_kernel, tiles_per_seq=seq // tm, lora_pad=lora_pad, lora_cols=(cw, ca, cg))
    row = lambda i: (i, 0)
    outs = pl.pallas_call(
        kern,
        grid=(n // tm,),
        in_specs=[
            pl.BlockSpec((tm, d), row),
            _resident((1, d)),
            _resident((2 * wm, d)), _resident(w_in.shape),
            _resident((d, lora_pad)), _resident((d, 3)), _resident((1, 3 * wm)),
        ],
        out_specs=[
            pl.BlockSpec((1, nblk, wm, MOBA_BLOCK), lambda i: (i, 0, 0, 0)),
            pl.BlockSpec((tm, wm), row),
            pl.BlockSpec((1, nblk, wm, MOBA_BLOCK), lambda i: (i, 0, 0, 0)),
            pl.BlockSpec((1, nblk, wm), lambda i: (i, 0, 0)),
            pl.BlockSpec((tm, 3 * wm), row),
            pl.BlockSpec((tm, 2 * d), row),
            pl.BlockSpec((tm, lora_pad), row),
        ],
        out_shape=[
            jax.ShapeDtypeStruct((n // tm, nblk, wm, MOBA_BLOCK), BF16),
            jax.ShapeDtypeStruct((n, wm), BF16),
            jax.ShapeDtypeStruct((n // tm, nblk, wm, MOBA_BLOCK), BF16),
            jax.ShapeDtypeStruct((n // tm, nblk, wm), F32),
            jax.ShapeDtypeStruct((n, 3 * wm), F32),
            jax.ShapeDtypeStruct((n, 2 * d), BF16),
            jax.ShapeDtypeStruct((n, lora_pad), F32),
        ],
        scratch_shapes=[
            pltpu.VMEM((d, 2 * lora_pad), BF16),
            pltpu.VMEM((1, 3 * wm), F32),
            pltpu.VMEM((1, lora_pad), F32),
        ],
        compiler_params=_params(1),
        name="inproj",
    )(x2d, gain.reshape(1, d), wqvt, wb, wla, mix_wag.T, mix_rkv.reshape(1, 3 * wm))
    return outs


def _t5_bucket_np(dist, num_buckets):
    n = np.maximum(dist, 0)
    max_exact = num_buckets // 2
    nf = np.maximum(n, max_exact).astype(np.float32)
    large = max_exact + (np.log(nf / np.float32(max_exact)) / np.float32(math.log(MAX_DISTANCE / max_exact))
                         * np.float32(num_buckets - max_exact)).astype(np.int32)
    large = np.minimum(large, num_buckets - 1)
    return np.where(n < max_exact, n, large).astype(np.int32)


def _bias_kernel(rb_ref, bkt_ref, o_ref, *, num_buckets, bucket_ranges):
    h = pl.program_id(0)
    last = rb_ref[num_buckets - 1, h]
    blk = bkt_ref.shape[1]
    causal = lax.broadcasted_iota(jnp.int32, (blk, blk), 0) <= lax.broadcasted_iota(jnp.int32, (blk, blk), 1)
    for o, (b_lo, b_hi) in enumerate(bucket_ranges):
        bkt = bkt_ref[o]
        acc = jnp.zeros(bkt.shape, F32)
        for b in range(b_lo, min(b_hi, num_buckets - 2) + 1):
            acc = jnp.where(bkt == b, rb_ref[b, h] - last, acc)
        acc = acc * LOG2E
        o_ref[0, o] = jnp.where(causal, acc, NEG_BIG) if o == 0 else acc


def _bias_tiles(rel_bias, n_heads):
    num_buckets = rel_bias.shape[0]
    blk = MOBA_BLOCK
    key = np.arange(blk)[:, None]
    qry = np.arange(blk)[None, :]
    offs = np.arange(NEAR_TILES)[:, None, None]
    bkt = _t5_bucket_np(offs * blk + qry - key, num_buckets)
    far = _t5_bucket_np(np.arange(NEAR_TILES * blk - blk + 1, 16 * blk * 64), num_buckets)
    assert (far == num_buckets - 1).all()
    ranges = tuple((int(bkt[o].min()), int(bkt[o].max())) for o in range(NEAR_TILES))
    return pl.pallas_call(
        functools.partial(_bias_kernel, num_buckets=num_buckets, bucket_ranges=ranges),
        grid=(n_heads,),
        in_specs=[
            pl.BlockSpec(memory_space=pltpu.SMEM),
            _resident((NEAR_TILES, blk, blk)),
        ],
        out_specs=pl.BlockSpec((1, NEAR_TILES, blk, blk), lambda h: (h, 0, 0, 0)),
        out_shape=jax.ShapeDtypeStruct((n_heads, NEAR_TILES, blk, blk), F32),
        compiler_params=_params(1),
        name="bias_tiles",
    )(rel_bias, jnp.asarray(bkt))


def _moba_kernel(qa_ref, qb_ref, k_ref, vt_ref, kmean_ref, bias_ref, hot_ref, o_ref, *, topk):
    p = pl.program_id(2)
    blk = MOBA_BLOCK
    nb = k_ref.shape[0]
    hd = HEAD_DIM
    grp = MOBA_GROUP
    i_a = p
    i_b = nb - 1 - p

    chan = lax.broadcasted_iota(jnp.int32, (LANES, blk), 0)
    head_rows = [chan < hd, chan >= hd]
    km = kmean_ref[...]
    km_hi = km.astype(BF16)
    km_lo = (km - km_hi.astype(F32)).astype(BF16)
    nidx = lax.broadcasted_iota(jnp.int32, (nb, blk), 0)
    ones_rows = jnp.ones((MOBA_ONES_ROWS, blk), BF16)
    dead_rows = jnp.full((LANES - nb, blk), NEG_BIG, BF16)

    def query_operands(qt_ref, i):
        qt2 = qt_ref[0, 0]
        out = []
        for e in range(2):
            qt_e = jnp.where(head_rows[e], qt2, jnp.zeros_like(qt2))
            gate = _dot(km_hi, qt_e) + _dot(km_lo, qt_e)
            valid = nidx < i
            gv = jnp.where(valid, gate, -jnp.inf)
            cnt = jnp.zeros((nb, blk), jnp.int32)
            for n2 in range(nb):
                row = gv[n2:n2 + 1, :]
                beats = (row > gv) | ((row == gv) & (n2 < nidx))
                cnt = cnt + beats.astype(jnp.int32)
            sel = valid & (cnt < topk)
            msk = jnp.where(sel, 0.0, NEG_BIG).astype(BF16)
            out.append((qt_e, jnp.concatenate([qt_e, msk, dead_rows], axis=0)))
        return out

    def scores(tile):
        who, qops, j, bias_idx = tile
        if bias_idx == 0:
            s = [_dot(k_ref[j], qops[e][0]) for e in range(2)]
        else:
            k_aug = jnp.concatenate([k_ref[jnp.maximum(j, 0)], hot_ref[jnp.where(j >= 0, j, nb)]], axis=1)
            s = [_dot(k_aug, qops[e][1]) for e in range(2)]
        if bias_idx is not None:
            s = [s[e] + bias_ref[e, bias_idx] for e in range(2)]
        return s

    def update(st, e, s, j):
        m, acc = st
        m_new = jnp.maximum(m, jnp.max(s, axis=0, keepdims=True))
        vt_t = jnp.concatenate([vt_ref[0, jnp.maximum(j, 0), e * hd:(e + 1) * hd, :], ones_rows], axis=0)
        return m_new, jnp.exp2(m - m_new) * acc + _dot(vt_t, jnp.exp2(s - m_new).astype(BF16))

    def past_tiles(who, qops, i, g, near):
        return [(who, qops, i - grp * (g + 1) + t, (grp - t) if near else None) for t in range(grp)]

    qa = query_operands(qa_ref, i_a)
    qb = query_operands(qb_ref, i_b)
    a_turn = i_a > grp
    qx = [(None, jnp.where(a_turn, qa[e][1], qb[e][1])) for e in range(2)]
    tiles = ([("b", qb, i_b, 0), ("a", qa, i_a, 0)]
             + past_tiles("b", qb, i_b, 0, True) + past_tiles("b", qb, i_b, 1, False)
             + past_tiles("a", qa, i_a, 0, True) + past_tiles("b", qb, i_b, 2, False)
             + past_tiles("x", qx, jnp.where(a_turn, i_a, i_b), jnp.where(a_turn, 1, 3), False))

    init = (jnp.full((1, blk), NEG_BIG, F32), jnp.zeros((hd + MOBA_ONES_ROWS, blk), F32))
    st = {"a": [init, init], "b": [init, init]}
    ahead = 2
    pending = [scores(t) for t in tiles[:ahead]]
    for n, tile in enumerate(tiles):
        if n + ahead < len(tiles):
            pending.append(scores(tiles[n + ahead]))
        s = pending.pop(0)
        who, _, j, _ = tile
        if who == "x" and "x" not in st:
            st["x"] = [tuple(jnp.where(a_turn, ua, ub) for ua, ub in zip(st["a"][e], st["b"][e])) for e in range(2)]
        st[who] = [update(st[who][e], e, s[e], j) for e in range(2)]
    st_a = [tuple(jnp.where(a_turn, ux, ua) for ux, ua in zip(st["x"][e], st["a"][e])) for e in range(2)]
    st_b = [tuple(jnp.where(a_turn, ub, ux) for ux, ub in zip(st["x"][e], st["b"][e])) for e in range(2)]

    for half, sth in enumerate((st_a, st_b)):
        out_t = jnp.concatenate([sth[e][1][:hd] / sth[e][1][hd:hd + 1] for e in range(2)], axis=0)
        o_ref[half * blk:(half + 1) * blk, :] = out_t.T.astype(BF16)


def _moba(qt, k, vt, kmean, bias, *, batch, seq):
    n, wm = k.shape
    blk = MOBA_BLOCK
    nb = seq // blk
    n_pairs = wm // LANES
    grp = MOBA_GROUP
    topk = min(MOBA_TOPK, max(nb - 1, 1))
    assert nb == 4 * grp and NEAR_TILES == grp + 1 and nb + 1 <= LANES
    k3 = k.reshape(n // blk, blk, wm)
    qt4 = qt.reshape(batch, nb, wm, blk)
    vt4 = vt.reshape(batch, nb, wm, blk)
    km2 = kmean.reshape(n // blk, wm)
    hot = np.zeros((nb + 1, blk, LANES), np.float32)
    hot[np.arange(nb + 1), :, np.arange(nb + 1)] = 1.0
    half = nb // 2
    return pl.pallas_call(
        functools.partial(_moba_kernel, topk=topk),
        grid=(batch, n_pairs, half),
        in_specs=[
            pl.BlockSpec((1, 1, LANES, blk), lambda b, hp, p: (b, p, hp, 0)),
            pl.BlockSpec((1, 1, LANES, blk), lambda b, hp, p: (b, nb - 1 - p, hp, 0)),
            pl.BlockSpec((nb, blk, LANES), lambda b, hp, p: (b, 0, hp)),
            pl.BlockSpec((1, nb, LANES, blk), lambda b, hp, p: (b, 0, hp, 0)),
            pl.BlockSpec((nb, LANES), lambda b, hp, p: (b, hp)),
            pl.BlockSpec((2, NEAR_TILES, blk, blk), lambda b, hp, p: (hp, 0, 0, 0)),
            _resident((nb + 1, blk, LANES)),
        ],
        out_specs=pl.BlockSpec((2 * blk, LANES), lambda b, hp, p: (b * half + p, hp)),
        out_shape=jax.ShapeDtypeStruct((n, wm), BF16),
        compiler_params=_params(3),
        name="moba",
    )(qt4, qt4, k3, vt4, km2, bias, jnp.asarray(hot, BF16))


def _head_sum(x, lo):
    out = []
    for q in range(x.shape[1] // LANES):
        xq = x[:, q * LANES:(q + 1) * LANES]
        zero = jnp.zeros_like(xq)
        s0 = jnp.sum(jnp.where(lo, xq, zero), axis=-1, keepdims=True)
        s1 = jnp.sum(jnp.where(lo, zero, xq), axis=-1, keepdims=True)
        out.append(jnp.where(lo, s0, s1))
    return out[0] if len(out) == 1 else jnp.concatenate(out, axis=1)


def _rwkv_kernel(r_ref, k_ref, v_ref, lp_ref, wlb_ref, alb_ref, glb_ref, par_ref, o_ref, z_scr, y_scr):
    t = pl.program_id(2)
    ts, width = r_ref.shape
    n_pairs = width // LANES
    c = RWKV_CHUNK
    hd = HEAD_DIM

    @pl.when(t == 0)
    def _():
        z_scr[...] = jnp.zeros_like(z_scr)

    lo = lax.broadcasted_iota(jnp.int32, (ts, LANES), 1) < hd

    par = par_ref[...]
    w0, a0, k_k, k_a, r_k, ln_w, ln_b = (par[j:j + 1, :] for j in range(7))

    lp = lp_ref[...]
    wa = lp[:, :LANES]
    wa_act = jnp.where(lo, jnp.tanh(wa), wa).astype(BF16)
    wpre = w0 + _dot(wa_act, wlb_ref[...])
    lw = jax.nn.sigmoid(wpre) * (-math.exp(-0.5))
    iclr = jax.nn.sigmoid(a0 + _dot(wa_act, alb_ref[...]))
    g = _dot(jax.nn.sigmoid(lp[:, LANES:]).astype(BF16), glb_ref[...])

    r = r_ref[...]
    k = k_ref[...]
    v = v_ref[...]
    kk = k * k_k
    kk = kk * lax.rsqrt(jnp.maximum(_head_sum(kk * kk, lo), 1e-24))
    k = k * (1.0 + (iclr - 1.0) * k_a)
    a_ = -kk
    b_ = kk * iclr
    bonus = _head_sum(r * k * r_k, lo) * v

    row_c = lax.broadcasted_iota(jnp.int32, (c, c), 0)
    col_c = lax.broadcasted_iota(jnp.int32, (c, c), 1)
    tri = jnp.where(row_c >= col_c, 1.0, 0.0).astype(BF16)
    eye = jnp.where(row_c == col_c, 1.0, 0.0)
    row_p = lax.broadcasted_iota(jnp.int32, (LANES, LANES), 0)
    col_p = lax.broadcasted_iota(jnp.int32, (LANES, LANES), 1)
    same_head = (row_p < hd) == (col_p < hd)
    diag_p = row_p == col_p
    lo_ar = lax.broadcasted_iota(jnp.int32, (2 * c, LANES), 1) < hd

    nc = ts // c
    chunks = [(q, ci) for q in range(n_pairs) for ci in range(nc)]
    units = [(q, ci, e) for (q, ci) in chunks for e in range(2)]
    row_w = lax.broadcasted_iota(jnp.int32, (c, LANES), 0)
    col_w = lax.broadcasted_iota(jnp.int32, (c, LANES), 1) & (c - 1)
    strict2 = row_w > col_w
    incl2 = row_w >= col_w
    zeros_cb = jnp.zeros((c, LANES), BF16)

    lw_hi = lw.astype(BF16)
    lw_mid = (lw - lw_hi.astype(F32)).astype(BF16)
    cums = [_dot(tri, lw_hi[ci * c:(ci + 1) * c]) + _dot(tri, lw_mid[ci * c:(ci + 1) * c])
            for ci in range(nc)]
    ch = {}
    for (q, ci) in chunks:
        sl = (slice(ci * c, (ci + 1) * c), slice(q * LANES, (q + 1) * LANES))
        rc, kc, vc, ac, bc, lwc, cum = r[sl], k[sl], v[sl], a_[sl], b_[sl], lw[sl], cums[ci][:, sl[1]]
        tot = cum[c - 1:c, :]
        e_neg = jnp.exp(-cum)
        e_fwd = jnp.exp(tot - cum)
        rt = rc * jnp.exp(cum)
        at = ac * jnp.exp(cum - lwc)
        vb = vc.astype(BF16)
        ch[q, ci] = dict(
            rt=rt, at=at, vc=vc, p_c=jnp.exp(tot),
            ar=jnp.concatenate([at, rt], axis=0),
            btkt=jnp.concatenate([bc * e_neg, kc * e_neg], axis=0).astype(BF16),
            bbkb=jnp.concatenate([bc * e_fwd, kc * e_fwd], axis=0),
            vpad=jnp.concatenate([zeros_cb, vb], axis=0),
            zv=jnp.concatenate([zeros_cb, vb], axis=1),
        )

    dtop, dbot = {}, {}
    for (q, ci, e) in units:
        ar_e = jnp.where(lo_ar if e == 0 else ~lo_ar, ch[q, ci]["ar"], 0.0).astype(BF16)
        d = _dot_nt(ar_e, ch[q, ci]["btkt"])
        dtop[q, ci, e] = jnp.where(strict2, d[:c], 0.0)
        dbot[q, ci, e] = jnp.where(incl2, d[c:], 0.0).astype(BF16)

    pw = {u: dtop[u][:, :c] for u in units}
    tm = {u: eye + pw[u] for u in units}
    for u in units:
        pb = pw[u].astype(BF16)
        pw[u] = _dot(pb, pb)
    for _ in range(int(math.log2(c)) - 2):
        for u in units:
            pb = pw[u].astype(BF16)
            sq = _dot(jnp.concatenate([pb, tm[u].astype(BF16)], axis=0), pb)
            pw[u] = sq[:c]
            tm[u] = tm[u] + sq[c:]
    for u in units:
        tm[u] = tm[u] + _dot(tm[u].astype(BF16), pw[u].astype(BF16))

    akv = {(q, ci, e): _dot(dtop[q, ci, e].astype(BF16), ch[q, ci]["vpad"]) for (q, ci, e) in units}
    wu = {(q, ci, e): _dot(tm[q, ci, e].astype(BF16),
                           jnp.concatenate([ch[q, ci]["at"], akv[q, ci, e]], axis=1).astype(BF16))
          for (q, ci, e) in units}
    qy = {(q, ci, e): _dot(dbot[q, ci, e], jnp.concatenate([wu[q, ci, e].astype(BF16), ch[q, ci]["zv"]], axis=0))
          for (q, ci, e) in units}

    m2s, g2s, qes, y0s = {}, {}, {}, {}
    lo_c2 = (lax.broadcasted_iota(jnp.int32, (c, 2 * LANES), 1) & (LANES - 1)) < hd
    for (q, ci) in chunks:
        wu2 = jnp.where(lo_c2, wu[q, ci, 0], wu[q, ci, 1])
        qy2 = jnp.where(lo_c2, qy[q, ci, 0], qy[q, ci, 1])
        bkt = ch[q, ci]["bbkb"].T.astype(BF16)
        rhs = jnp.concatenate([wu2.astype(BF16), ch[q, ci]["zv"]], axis=0)
        mg = _dot(bkt, rhs)
        m2s[q, ci] = (jnp.where(same_head, mg[:, :LANES], 0.0)
                      + jnp.where(diag_p, ch[q, ci]["p_c"], 0.0)).astype(BF16)
        g2s[q, ci] = jnp.where(same_head, mg[:, LANES:], 0.0)
        qes[q, ci] = (ch[q, ci]["rt"] + qy2[:, :LANES]).astype(BF16)
        y0s[q, ci] = qy2[:, LANES:]

    z = [z_scr[q] for q in range(n_pairs)]
    for ci in range(nc):
        for q in range(n_pairs):
            zb = z[q].astype(BF16)
            y_scr[ci * c:(ci + 1) * c, q * LANES:(q + 1) * LANES] = _dot(qes[q, ci], zb) + y0s[q, ci]
            z[q] = _dot(m2s[q, ci], zb) + g2s[q, ci]
    for q in range(n_pairs):
        z_scr[q] = z[q]

    y = y_scr[...]
    mu = _head_sum(y, lo) * (1.0 / hd)
    dy = y - mu
    var = _head_sum(dy * dy, lo) * (1.0 / hd)
    yn = dy * lax.rsqrt(var + GN_EPS) * ln_w + ln_b
    o_ref[...] = ((yn + bonus) * g).astype(BF16)


def _rwkv(rkv, lora, w_lora_b, a_lora_b, g_lora_b, w0, a0, k_k, k_a, r_k, ln_x_w, ln_x_b, *, batch, seq):
    n = rkv.shape[0]
    wr = rkv.shape[1] // 3
    n_pairs = wr // LANES
    ts = TOKEN_TILE
    nt = seq // ts
    cw, ca, cg = w_lora_b.shape[0], a_lora_b.shape[0], g_lora_b.shape[0]
    lora_pad = lora.shape[1]
    assert cw == HEAD_DIM and ca == HEAD_DIM and cw + ca == LANES
    wlb = jnp.concatenate([w_lora_b, jnp.zeros((LANES - cw, wr), F32)], axis=0).astype(BF16)
    alb = jnp.concatenate([jnp.zeros((LANES - ca, wr), F32), a_lora_b], axis=0).astype(BF16)
    glb = jnp.concatenate([g_lora_b, jnp.zeros((lora_pad - LANES - cg, wr), F32)], axis=0).astype(BF16)
    par = jnp.stack([w0, a0, k_k, k_a, r_k.reshape(-1), ln_x_w, ln_x_b, jnp.zeros_like(w0)], axis=0)
    width = RWKV_PAIRS * LANES
    n_groups = wr // width
    tok = lambda off: (lambda b, p, t: (b * nt + t, off + p))
    col = lambda b, p, t: (0, p)
    return pl.pallas_call(
        _rwkv_kernel,
        grid=(batch, n_groups, nt),
        in_specs=[
            pl.BlockSpec((ts, width), tok(0)),
            pl.BlockSpec((ts, width), tok(n_groups)),
            pl.BlockSpec((ts, width), tok(2 * n_groups)),
            pl.BlockSpec((ts, lora_pad), lambda b, p, t: (b * nt + t, 0)),
            pl.BlockSpec((LANES, width), col),
            pl.BlockSpec((LANES, width), col),
            pl.BlockSpec((lora_pad - LANES, width), col),
            pl.BlockSpec((8, width), col),
        ],
        out_specs=pl.BlockSpec((ts, width), tok(0)),
        out_shape=jax.ShapeDtypeStruct((n, wr), BF16),
        scratch_shapes=[pltpu.VMEM((RWKV_PAIRS, LANES, LANES), F32), pltpu.VMEM((ts, width), F32)],
        compiler_params=_params(3),
        name="rwkv7",
    )(rkv, rkv, rkv, lora, wlb, alb, glb, par)


def _merge_kernel(x_ref, ya0_ref, ya1_ref, yb_ref, gate_ref, pa_ref, pb_ref, wo_ref, o_ref):
    d = x_ref.shape[1]
    gate = gate_ref[...].astype(F32)
    ya = jnp.concatenate([ya0_ref[...], ya1_ref[...]], axis=0)
    merged = (jax.nn.sigmoid(gate[:, :d]) * _dot(ya, pa_ref[...])
              + jax.nn.sigmoid(gate[:, d:]) * _dot(yb_ref[...], pb_ref[...]))
    o_ref[...] = x_ref[...] + _dot(merged.astype(BF16), wo_ref[...])


def _merge(x2d, ya, yb, gate, proj_a, proj_b, w_out, *, seq):
    n, d = x2d.shape
    wm = ya.shape[1]
    tm = 2 * MOBA_BLOCK
    nb = seq // MOBA_BLOCK
    tiles_per_seq = seq // tm
    row = lambda i: (i, 0)

    def ya_block(which):
        def index(i):
            blk_i = 2 * (i % tiles_per_seq) + which
            pos = jnp.where(blk_i < nb // 2, 2 * blk_i, 2 * (nb - 1 - blk_i) + 1)
            return ((i // tiles_per_seq) * nb + pos, 0)
        return pl.BlockSpec((MOBA_BLOCK, wm), index)

    return pl.pallas_call(
        _merge_kernel,
        grid=(n // tm,),
        in_specs=[
            pl.BlockSpec((tm, d), row), ya_block(0), ya_block(1), pl.BlockSpec((tm, wm), row),
            pl.BlockSpec((tm, 2 * d), row),
            _resident((wm, d)), _resident((wm, d)), _resident((d, d)),
        ],
        out_specs=pl.BlockSpec((tm, d), row),
        out_shape=jax.ShapeDtypeStruct((n, d), F32),
        compiler_params=_params(1),
        name="merge",
    )(x2d, ya, ya, yb, gate, proj_a.astype(BF16), proj_b.astype(BF16), w_out.astype(BF16))


def kernel(x, rel_bias, norm_ffn1, ffn1_gate, ffn1_up, ffn1_down, norm_mix, w_in, mix_rkv, mix_wag, w_lora_a, w_lora_b, w0, a_lora_a, a_lora_b, a0, g_lora_a, g_lora_b, k_k, k_a, r_k, ln_x_w, ln_x_b, proj_a, proj_b, w_out, norm_ffn2, ffn2_gate, ffn2_up, ffn2_down, norm_final):
    batch, seq, d = x.shape
    depth = norm_ffn1.shape[0]
    assert depth >= 1
    n_heads = rel_bias.shape[1]
    bias = _bias_tiles(rel_bias, n_heads)
    x2d = x.reshape(batch * seq, d)
    for l in range(depth):
        x2d = _ffn(x2d, norm_ffn1[l], ffn1_gate[l], ffn1_up[l], ffn1_down[l], norm_final, final_norm=False)
        qt, k, vt, kmean, rkv, gate, lora = _inproj(
            x2d, norm_mix[l], w_in[l], mix_rkv[l], mix_wag[l], w_lora_a[l], a_lora_a[l], g_lora_a[l],
            batch=batch, seq=seq)
        ya = _moba(qt, k, vt, kmean, bias, batch=batch, seq=seq)
        yb = _rwkv(rkv, lora, w_lora_b[l], a_lora_b[l], g_lora_b[l], w0[l], a0[l], k_k[l], k_a[l], r_k[l],
                   ln_x_w[l], ln_x_b[l], batch=batch, seq=seq)
        x2d = _merge(x2d, ya, yb, gate, proj_a[l], proj_b[l], w_out[l], seq=seq)
        x2d = _ffn(x2d, norm_ffn2[l], ffn2_gate[l], ffn2_up[l], ffn2_down[l], norm_final,
                   final_norm=(l == depth - 1))
    return x2d.reshape(batch, seq, d)
```

```python
import functools
import math

import numpy as np
import jax
import jax.numpy as jnp
from jax import lax
from jax.experimental import pallas as pl
from jax.experimental.pallas import tpu as pltpu

F32 = jnp.float32
BF16 = jnp.bfloat16

HEAD_DIM = 64
LANES = 128
MOBA_BLOCK = 256
MOBA_TOPK = 3
MOBA_GROUP = 4
MOBA_ONES_ROWS = 16
MAX_DISTANCE = 1024
RMS_EPS = 1e-6
GN_EPS = HEAD_DIM * 1e-5
RWKV_CHUNK = 64
MOBA_TILE_COST = 336
MOBA_STEP_COST = 24 * MOBA_TILE_COST
RWKV_STEP_COST = 38000
NEG_BIG = -1e30
LOG2E = math.log2(math.e)
NEAR_TILES = 5
TOKEN_TILE = 512
FFN_CHUNK = 256
VMEM_LIMIT = 56 * 1024 * 1024

_NT = (((1,), (1,)), ((), ()))


def _dot(a, b):
    return jnp.dot(a, b, preferred_element_type=F32)


def _dot_nt(a, b):
    return lax.dot_general(a, b, _NT, preferred_element_type=F32)


def _rms(x, gain):
    ms = jnp.mean(x * x, axis=-1, keepdims=True)
    return x * lax.rsqrt(ms + RMS_EPS) * gain


def _resident(shape):
    nd = len(shape)
    return pl.BlockSpec(shape, lambda *_: (0,) * nd, pipeline_mode=pl.Buffered(1))


def _params(n_axes):
    return pltpu.CompilerParams(dimension_semantics=("arbitrary",) * n_axes, vmem_limit_bytes=VMEM_LIMIT)


def _ffn_kernel(x_ref, gain_ref, wg_ref, wu_ref, wd_ref, fgain_ref, o_ref, act_ref, *, n_chunks, fc, final_norm):
    x = x_ref[...]
    h = _rms(x, gain_ref[...]).astype(BF16)
    for c in range(n_chunks):
        g = _dot(h, wg_ref[:, c * fc:(c + 1) * fc])
        u = _dot(h, wu_ref[:, c * fc:(c + 1) * fc])
        act_ref[:, c * fc:(c + 1) * fc] = (g * jax.nn.sigmoid(g) * u).astype(BF16)
    o = x + 0.5 * _dot(act_ref[...], wd_ref[...])
    if final_norm:
        o = _rms(o, fgain_ref[...])
    o_ref[...] = o


def _ffn(x2d, gain, w_gate, w_up, w_down, final_gain, *, final_norm):
    n, d = x2d.shape
    f = w_gate.shape[1]
    fc = FFN_CHUNK
    n_chunks = f // fc
    tm = TOKEN_TILE
    kern = functools.partial(_ffn_kernel, n_chunks=n_chunks, fc=fc, final_norm=final_norm)
    return pl.pallas_call(
        kern,
        grid=(n // tm,),
        in_specs=[
            pl.BlockSpec((tm, d), lambda i: (i, 0)),
            _resident((1, d)),
            _resident((d, f)),
            _resident((d, f)),
            _resident((f, d)),
            _resident((1, d)),
        ],
        out_specs=pl.BlockSpec((tm, d), lambda i: (i, 0)),
        out_shape=jax.ShapeDtypeStruct((n, d), F32),
        scratch_shapes=[pltpu.VMEM((tm, f), BF16)],
        compiler_params=_params(1),
        name="ffn_final" if final_norm else "ffn",
    )(x2d, gain.reshape(1, d), w_gate.astype(BF16), w_up.astype(BF16), w_down.astype(BF16),
      final_gain.reshape(1, d))


def _shift_rows(z, carry_row):
    rolled = pltpu.roll(z, 1, 0)
    row = lax.broadcasted_iota(jnp.int32, z.shape, 0)
    return jnp.where(row == 0, carry_row, rolled)


def _inproj_kernel(x_ref, gain_ref, wqvt_ref, w_ref, wla_ref, mixt_ref, mixrkv_ref,
                   qt_ref, k_ref, vt_ref, kmean_ref, rkv_ref, gate_ref, lora_ref,
                   wl_scr, carry_rkv, carry_lora, *, tiles_per_seq, lora_pad, lora_cols):
    i = pl.program_id(0)
    tm = x_ref.shape[0]
    wm = k_ref.shape[1]
    cw, ca, cg = lora_cols

    @pl.when(i == 0)
    def _():
        col = lax.broadcasted_iota(jnp.int32, (wla_ref.shape[0], lora_pad), 1)
        m = jnp.where(col < cw, mixt_ref[:, 0:1], jnp.where(col < cw + ca, mixt_ref[:, 1:2], mixt_ref[:, 2:3]))
        wla = wla_ref[...]
        wl_scr[:, :lora_pad] = ((1.0 - m) * wla).astype(BF16)
        wl_scr[:, lora_pad:] = (m * wla).astype(BF16)

    @pl.when(i % tiles_per_seq == 0)
    def _():
        carry_rkv[...] = jnp.zeros_like(carry_rkv)
        carry_lora[...] = jnp.zeros_like(carry_lora)

    h = _rms(x_ref[...], gain_ref[...]).astype(BF16)

    kf = _dot(h, w_ref[:, wm:2 * wm])
    k_ref[...] = kf.astype(BF16)
    for j in range(tm // MOBA_BLOCK):
        kmean_ref[0, j:j + 1, :] = jnp.mean(kf[j * MOBA_BLOCK:(j + 1) * MOBA_BLOCK], axis=0, keepdims=True)

    rkv = _dot(h, w_ref[:, 3 * wm:6 * wm])
    prev = _shift_rows(rkv, carry_rkv[...])
    carry_rkv[...] = rkv[tm - 1:tm, :]
    rkv_ref[...] = rkv + (prev - rkv) * mixrkv_ref[...]

    gate_ref[...] = _dot(h, w_ref[:, 6 * wm:]).astype(BF16)

    lo = _dot(h, wl_scr[...])
    l1 = lo[:, :lora_pad]
    l2 = lo[:, lora_pad:]
    lora_ref[...] = l1 + _shift_rows(l2, carry_lora[...])
    carry_lora[...] = l2[tm - 1:tm, :]

    qvt = _dot_nt(wqvt_ref[...], h)
    for j in range(tm // MOBA_BLOCK):
        cols = slice(j * MOBA_BLOCK, (j + 1) * MOBA_BLOCK)
        qt_ref[0, j] = (qvt[:wm, cols] * (HEAD_DIM ** -0.5 * LOG2E)).astype(BF16)
        vt_ref[0, j] = qvt[wm:, cols].astype(BF16)


def _inproj(x2d, gain, w_in, mix_rkv, mix_wag, w_lora_a, a_lora_a, g_lora_a, *, batch, seq):
    n, d = x2d.shape
    tm = TOKEN_TILE
    wm = (w_in.shape[1] - 2 * d) // 6
    nblk = tm // MOBA_BLOCK
    cw, ca, cg = w_lora_a.shape[1], a_lora_a.shape[1], g_lora_a.shape[1]
    lora_pad = -(-(cw + ca + cg) // LANES) * LANES
    wb = w_in.astype(BF16)
    wqvt = jnp.concatenate([wb[:, :wm], wb[:, 2 * wm:3 * wm]], axis=1).T
    wla = jnp.concatenate([w_lora_a, a_lora_a, g_lora_a, jnp.zeros((d, lora_pad - cw - ca - cg), F32)], axis=1)
    kern = functools.partial(_inproj_kernel, tiles_per_seq=seq // tm, lora_pad=lora_pad, lora_cols=(cw, ca, cg))
    row = lambda i: (i, 0)
    outs = pl.pallas_call(
        kern,
        grid=(n // tm,),
        in_specs=[
            pl.BlockSpec((tm, d), row),
            _resident((1, d)),
            _resident((2 * wm, d)), _resident(w_in.shape),
            _resident((d, lora_pad)), _resident((d, 3)), _resident((1, 3 * wm)),
        ],
        out_specs=[
            pl.BlockSpec((1, nblk, wm, MOBA_BLOCK), lambda i: (i, 0, 0, 0)),
            pl.BlockSpec((tm, wm), row),
            pl.BlockSpec((1, nblk, wm, MOBA_BLOCK), lambda i: (i, 0, 0, 0)),
            pl.BlockSpec((1, nblk, wm), lambda i: (i, 0, 0)),
            pl.BlockSpec((tm, 3 * wm), row),
            pl.BlockSpec((tm, 2 * d), row),
            pl.BlockSpec((tm, lora_pad), row),
        ],
        out_shape=[
            jax.ShapeDtypeStruct((n // tm, nblk, wm, MOBA_BLOCK), BF16),
            jax.ShapeDtypeStruct((n, wm), BF16),
            jax.ShapeDtypeStruct((n // tm, nblk, wm, MOBA_BLOCK), BF16),
            jax.ShapeDtypeStruct((n // tm, nblk, wm), F32),
            jax.ShapeDtypeStruct((n, 3 * wm), F32),
            jax.ShapeDtypeStruct((n, 2 * d), BF16),
            jax.ShapeDtypeStruct((n, lora_pad), F32),
        ],
        scratch_shapes=[
            pltpu.VMEM((d, 2 * lora_pad), BF16),
            pltpu.VMEM((1, 3 * wm), F32),
            pltpu.VMEM((1, lora_pad), F32),
        ],
        compiler_params=_params(1),
        name="inproj",
    )(x2d, gain.reshape(1, d), wqvt, wb, wla, mix_wag.T, mix_rkv.reshape(1, 3 * wm))
    return outs


def _t5_bucket_np(dist, num_buckets):
    n = np.maximum(dist, 0)
    max_exact = num_buckets // 2
    nf = np.maximum(n, max_exact).astype(np.float32)
    large = max_exact + (np.log(nf / np.float32(max_exact)) / np.float32(math.log(MAX_DISTANCE / max_exact))
                         * np.float32(num_buckets - max_exact)).astype(np.int32)
    large = np.minimum(large, num_buckets - 1)
    return np.where(n < max_exact, n, large).astype(np.int32)


def _bias_kernel(rb_ref, bkt_ref, o_ref, *, num_buckets, bucket_ranges):
    h = pl.program_id(0)
    last = rb_ref[num_buckets - 1, h]
    blk = bkt_ref.shape[1]
    causal = lax.broadcasted_iota(jnp.int32, (blk, blk), 0) <= lax.broadcasted_iota(jnp.int32, (blk, blk), 1)
    for o, (b_lo, b_hi) in enumerate(bucket_ranges):
        bkt = bkt_ref[o]
        acc = jnp.zeros(bkt.shape, F32)
        for b in range(b_lo, min(b_hi, num_buckets - 2) + 1):
            acc = jnp.where(bkt == b, rb_ref[b, h] - last, acc)
        acc = acc * LOG2E
        o_ref[0, o] = jnp.where(causal, acc, NEG_BIG) if o == 0 else acc


def _bias_tiles(rel_bias, n_heads):
    num_buckets = rel_bias.shape[0]
    blk = MOBA_BLOCK
    key = np.arange(blk)[:, None]
    qry = np.arange(blk)[None, :]
    offs = np.arange(NEAR_TILES)[:, None, None]
    bkt = _t5_bucket_np(offs * blk + qry - key, num_buckets)
    far = _t5_bucket_np(np.arange(NEAR_TILES * blk - blk + 1, 16 * blk * 64), num_buckets)
    assert (far == num_buckets - 1).all()
    ranges = tuple((int(bkt[o].min()), int(bkt[o].max())) for o in range(NEAR_TILES))
    return pl.pallas_call(
        functools.partial(_bias_kernel, num_buckets=num_buckets, bucket_ranges=ranges),
        grid=(n_heads,),
        in_specs=[
            pl.BlockSpec(memory_space=pltpu.SMEM),
            _resident((NEAR_TILES, blk, blk)),
        ],
        out_specs=pl.BlockSpec((1, NEAR_TILES, blk, blk), lambda h: (h, 0, 0, 0)),
        out_shape=jax.ShapeDtypeStruct((n_heads, NEAR_TILES, blk, blk), F32),
        compiler_params=_params(1),
        name="bias_tiles",
    )(rel_bias, jnp.asarray(bkt))


def _moba_steps(p, hp, qa_ref, qb_ref, k_ref, vt_ref, kmean_ref, bias_ref, hot_ref, o_ref, topk):
    blk = MOBA_BLOCK
    nb = k_ref.shape[0]
    hd = HEAD_DIM
    grp = MOBA_GROUP
    i_a = p
    i_b = nb - 1 - p
    lanes = slice(hp * LANES, (hp + 1) * LANES)

    chan = lax.broadcasted_iota(jnp.int32, (LANES, blk), 0)
    head_rows = [chan < hd, chan >= hd]
    km = kmean_ref[:, lanes]
    km_hi = km.astype(BF16)
    km_lo = (km - km_hi.astype(F32)).astype(BF16)
    nidx = lax.broadcasted_iota(jnp.int32, (nb, blk), 0)
    ones_rows = jnp.ones((MOBA_ONES_ROWS, blk), BF16)
    dead_rows = jnp.full((LANES - nb, blk), NEG_BIG, BF16)

    def query_operands(qt_ref, i):
        qt2 = qt_ref[0, 0, lanes, :]
        out = []
        for e in range(2):
            qt_e = jnp.where(head_rows[e], qt2, jnp.zeros_like(qt2))
            gate = _dot(km_hi, qt_e) + _dot(km_lo, qt_e)
            valid = nidx < i
            gv = jnp.where(valid, gate, -jnp.inf)
            cnt = jnp.zeros((nb, blk), jnp.int32)
            for n2 in range(nb):
                row = gv[n2:n2 + 1, :]
                beats = (row > gv) | ((row == gv) & (n2 < nidx))
                cnt = cnt + beats.astype(jnp.int32)
            sel = valid & (cnt < topk)
            msk = jnp.where(sel, 0.0, NEG_BIG).astype(BF16)
            out.append((qt_e, jnp.concatenate([qt_e, msk, dead_rows], axis=0)))
        return out

    def scores(tile):
        who, qops, j, bias_idx = tile
        if bias_idx == 0:
            s = [_dot(k_ref[j, :, lanes], qops[e][0]) for e in range(2)]
        else:
            k_aug = jnp.concatenate([k_ref[jnp.maximum(j, 0), :, lanes], hot_ref[jnp.where(j >= 0, j, nb)]], axis=1)
            s = [_dot(k_aug, qops[e][1]) for e in range(2)]
        if bias_idx is not None:
            s = [s[e] + bias_ref[2 * hp + e, bias_idx] for e in range(2)]
        return s

    def update(st, e, s, j):
        m, acc = st
        m_new = jnp.maximum(m, jnp.max(s, axis=0, keepdims=True))
        rows = slice(hp * LANES + e * hd, hp * LANES + (e + 1) * hd)
        vt_t = jnp.concatenate([vt_ref[0, jnp.maximum(j, 0), rows, :], ones_rows], axis=0)
        return m_new, jnp.exp2(m - m_new) * acc + _dot(vt_t, jnp.exp2(s - m_new).astype(BF16))

    def past_tiles(who, qops, i, g, near):
        return [(who, qops, i - grp * (g + 1) + t, (grp - t) if near else None) for t in range(grp)]

    qa = query_operands(qa_ref, i_a)
    qb = query_operands(qb_ref, i_b)
    yield MOBA_TILE_COST
    a_turn = i_a > grp
    qx = [(None, jnp.where(a_turn, qa[e][1], qb[e][1])) for e in range(2)]
    tiles = ([("b", qb, i_b, 0), ("a", qa, i_a, 0)]
             + past_tiles("b", qb, i_b, 0, True) + past_tiles("b", qb, i_b, 1, False)
             + past_tiles("a", qa, i_a, 0, True) + past_tiles("b", qb, i_b, 2, False)
             + past_tiles("x", qx, jnp.where(a_turn, i_a, i_b), jnp.where(a_turn, 1, 3), False))

    init = (jnp.full((1, blk), NEG_BIG, F32), jnp.zeros((hd + MOBA_ONES_ROWS, blk), F32))
    st = {"a": [init, init], "b": [init, init]}
    ahead = 4
    pending = [scores(t) for t in tiles[:ahead]]
    for n, tile in enumerate(tiles):
        if n + ahead < len(tiles):
            pending.append(scores(tiles[n + ahead]))
        s = pending.pop(0)
        who, _, j, _ = tile
        if who == "x" and "x" not in st:
            st["x"] = [tuple(jnp.where(a_turn, ua, ub) for ua, ub in zip(st["a"][e], st["b"][e])) for e in range(2)]
        st[who] = [update(st[who][e], e, s[e], j) for e in range(2)]
        yield MOBA_TILE_COST
    st_a = [tuple(jnp.where(a_turn, ux, ua) for ux, ua in zip(st["x"][e], st["a"][e])) for e in range(2)]
    st_b = [tuple(jnp.where(a_turn, ub, ux) for ux, ub in zip(st["x"][e], st["b"][e])) for e in range(2)]

    for half, sth in enumerate((st_a, st_b)):
        out_t = jnp.concatenate([sth[e][1][:hd] / sth[e][1][hd:hd + 1] for e in range(2)], axis=0)
        o_ref[half * blk:(half + 1) * blk, lanes] = out_t.T.astype(BF16)


def _head_sum(x, lo):
    out = []
    for q in range(x.shape[1] // LANES):
        xq = x[:, q * LANES:(q + 1) * LANES]
        zero = jnp.zeros_like(xq)
        s0 = jnp.sum(jnp.where(lo, xq, zero), axis=-1, keepdims=True)
        s1 = jnp.sum(jnp.where(lo, zero, xq), axis=-1, keepdims=True)
        out.append(jnp.where(lo, s0, s1))
    return out[0] if len(out) == 1 else jnp.concatenate(out, axis=1)


def _rwkv_steps(r_ref, k_ref, v_ref, lp_ref, wlb_ref, alb_ref, glb_ref, par_ref, o_ref, z_scr, y_scr):
    ts, width = r_ref.shape
    n_pairs = width // LANES
    c = RWKV_CHUNK
    hd = HEAD_DIM

    lo = lax.broadcasted_iota(jnp.int32, (ts, LANES), 1) < hd

    par = par_ref[...]
    w0, a0, k_k, k_a, r_k, ln_w, ln_b = (par[j:j + 1, :] for j in range(7))

    lp = lp_ref[...]
    wa = lp[:, :LANES]
    wa_act = jnp.where(lo, jnp.tanh(wa), wa).astype(BF16)
    wpre = w0 + _dot(wa_act, wlb_ref[...])
    lw = jax.nn.sigmoid(wpre) * (-math.exp(-0.5))
    iclr = jax.nn.sigmoid(a0 + _dot(wa_act, alb_ref[...]))
    g = _dot(jax.nn.sigmoid(lp[:, LANES:]).astype(BF16), glb_ref[...])

    r = r_ref[...]
    k = k_ref[...]
    v = v_ref[...]
    kk = k * k_k
    kk = kk * lax.rsqrt(jnp.maximum(_head_sum(kk * kk, lo), 1e-24))
    k = k * (1.0 + (iclr - 1.0) * k_a)
    a_ = -kk
    b_ = kk * iclr
    bonus = _head_sum(r * k * r_k, lo) * v

    row_c = lax.broadcasted_iota(jnp.int32, (c, c), 0)
    col_c = lax.broadcasted_iota(jnp.int32, (c, c), 1)
    tri = jnp.where(row_c >= col_c, 1.0, 0.0).astype(BF16)
    eye = jnp.where(row_c == col_c, 1.0, 0.0)
    row_p = lax.broadcasted_iota(jnp.int32, (LANES, LANES), 0)
    col_p = lax.broadcasted_iota(jnp.int32, (LANES, LANES), 1)
    same_head = (row_p < hd) == (col_p < hd)
    diag_p = row_p == col_p
    lo_ar = lax.broadcasted_iota(jnp.int32, (2 * c, LANES), 1) < hd

    nc = ts // c
    chunks = [(q, ci) for q in range(n_pairs) for ci in range(nc)]
    units = [(q, ci, e) for (q, ci) in chunks for e in range(2)]
    row_w = lax.broadcasted_iota(jnp.int32, (c, LANES), 0)
    col_w = lax.broadcasted_iota(jnp.int32, (c, LANES), 1) & (c - 1)
    strict2 = row_w > col_w
    incl2 = row_w >= col_w
    zeros_cb = jnp.zeros((c, LANES), BF16)

    def per_pair(fn, cost):
        for q in range(n_pairs):
            for ci in range(nc):
                for e in range(2):
                    fn(q, ci, e)
            yield cost * nc * 2

    lw_hi = lw.astype(BF16)
    lw_mid = (lw - lw_hi.astype(F32)).astype(BF16)
    cums = [_dot(tri, lw_hi[ci * c:(ci + 1) * c]) + _dot(tri, lw_mid[ci * c:(ci + 1) * c])
            for ci in range(nc)]
    yield 32 * n_pairs * 2 * nc
    ch = {}
    for (q, ci) in chunks:
        sl = (slice(ci * c, (ci + 1) * c), slice(q * LANES, (q + 1) * LANES))
        rc, kc, vc, ac, bc, lwc, cum = r[sl], k[sl], v[sl], a_[sl], b_[sl], lw[sl], cums[ci][:, sl[1]]
        tot = cum[c - 1:c, :]
        e_neg = jnp.exp(-cum)
        e_fwd = jnp.exp(tot - cum)
        rt = rc * jnp.exp(cum)
        at = ac * jnp.exp(cum - lwc)
        vb = vc.astype(BF16)
        ch[q, ci] = dict(
            rt=rt, at=at, vc=vc, p_c=jnp.exp(tot),
            ar=jnp.concatenate([at, rt], axis=0),
            btkt=jnp.concatenate([bc * e_neg, kc * e_neg], axis=0).astype(BF16),
            bbkb=jnp.concatenate([bc * e_fwd, kc * e_fwd], axis=0),
            vpad=jnp.concatenate([zeros_cb, vb], axis=0),
            zv=jnp.concatenate([zeros_cb, vb], axis=1),
        )

    dtop, dbot, pw, tm, akv, wu, qy = {}, {}, {}, {}, {}, {}, {}

    def stage_d(q, ci, e):
        ar_e = jnp.where(lo_ar if e == 0 else ~lo_ar, ch[q, ci]["ar"], 0.0).astype(BF16)
        d = _dot_nt(ar_e, ch[q, ci]["btkt"])
        dtop[q, ci, e] = jnp.where(strict2, d[:c], 0.0)
        dbot[q, ci, e] = jnp.where(incl2, d[c:], 0.0).astype(BF16)
        pw[q, ci, e] = dtop[q, ci, e][:, :c]
        tm[q, ci, e] = eye + pw[q, ci, e]

    def stage_square(*u):
        pb = pw[u].astype(BF16)
        pw[u] = _dot(pb, pb)

    def stage_stacked(*u):
        pb = pw[u].astype(BF16)
        sq = _dot(jnp.concatenate([pb, tm[u].astype(BF16)], axis=0), pb)
        pw[u] = sq[:c]
        tm[u] = tm[u] + sq[c:]

    def stage_last(*u):
        tm[u] = tm[u] + _dot(tm[u].astype(BF16), pw[u].astype(BF16))

    def stage_akv(q, ci, e):
        akv[q, ci, e] = _dot(dtop[q, ci, e].astype(BF16), ch[q, ci]["vpad"])

    def stage_wu(q, ci, e):
        wu[q, ci, e] = _dot(tm[q, ci, e].astype(BF16),
                            jnp.concatenate([ch[q, ci]["at"], akv[q, ci, e]], axis=1).astype(BF16))

    def stage_qy(q, ci, e):
        qy[q, ci, e] = _dot(dbot[q, ci, e], jnp.concatenate([wu[q, ci, e].astype(BF16), ch[q, ci]["zv"]], axis=0))

    yield from per_pair(stage_d, 64)
    yield from per_pair(stage_square, 32)
    for _ in range(int(math.log2(c)) - 2):
        yield from per_pair(stage_stacked, 64)
    yield from per_pair(stage_last, 32)
    yield from per_pair(stage_akv, 32)
    yield from per_pair(stage_wu, 32)
    yield from per_pair(stage_qy, 32)

    m2s, g2s, qes, y0s = {}, {}, {}, {}
    lo_c2 = (lax.broadcasted_iota(jnp.int32, (c, 2 * LANES), 1) & (LANES - 1)) < hd
    for q in range(n_pairs):
        for ci in range(nc):
            wu2 = jnp.where(lo_c2, wu[q, ci, 0], wu[q, ci, 1])
            qy2 = jnp.where(lo_c2, qy[q, ci, 0], qy[q, ci, 1])
            bkt = ch[q, ci]["bbkb"].T.astype(BF16)
            rhs = jnp.concatenate([wu2.astype(BF16), ch[q, ci]["zv"]], axis=0)
            mg = _dot(bkt, rhs)
            m2s[q, ci] = (jnp.where(same_head, mg[:, :LANES], 0.0)
                          + jnp.where(diag_p, ch[q, ci]["p_c"], 0.0)).astype(BF16)
            g2s[q, ci] = jnp.where(same_head, mg[:, LANES:], 0.0)
            qes[q, ci] = (ch[q, ci]["rt"] + qy2[:, :LANES]).astype(BF16)
            y0s[q, ci] = qy2[:, LANES:]
        yield 64 * nc

    z = [z_scr[q] for q in range(n_pairs)]
    for ci in range(nc):
        for q in range(n_pairs):
            zb = z[q].astype(BF16)
            y_scr[ci * c:(ci + 1) * c, q * LANES:(q + 1) * LANES] = _dot(qes[q, ci], zb) + y0s[q, ci]
            z[q] = _dot(m2s[q, ci], zb) + g2s[q, ci]
        yield 96 * n_pairs
    for q in range(n_pairs):
        z_scr[q] = z[q]

    y = y_scr[...]
    mu = _head_sum(y, lo) * (1.0 / hd)
    dy = y - mu
    var = _head_sum(dy * dy, lo) * (1.0 / hd)
    yn = dy * lax.rsqrt(var + GN_EPS) * ln_w + ln_b
    o_ref[...] = ((yn + bonus) * g).astype(BF16)


def _interleave(gens_with_cost):
    state = [[g, 0.0, total] for g, total in gens_with_cost]
    while state:
        entry = min(state, key=lambda s: s[1] / s[2])
        try:
            entry[1] += next(entry[0])
        except StopIteration:
            state.remove(entry)


def _mixer_kernel(qa_ref, qb_ref, k_ref, vt_ref, kmean_ref, bias_ref, hot_ref,
                  r_ref, rk_ref, rv_ref, lp_ref, wlb_ref, alb_ref, glb_ref, par_ref,
                  ya_ref, yb_ref, z_scr, y_scr, *, topk):
    t = pl.program_id(1)

    @pl.when(t == 0)
    def _():
        z_scr[...] = jnp.zeros_like(z_scr)

    n_pairs = k_ref.shape[2] // LANES
    gens = [(_rwkv_steps(r_ref, rk_ref, rv_ref, lp_ref, wlb_ref, alb_ref, glb_ref, par_ref, yb_ref, z_scr, y_scr),
             RWKV_STEP_COST)]
    gens += [(_moba_steps(t, hp, qa_ref, qb_ref, k_ref, vt_ref, kmean_ref, bias_ref, hot_ref, ya_ref, topk),
              MOBA_STEP_COST) for hp in range(n_pairs)]
    _interleave(gens)


def _mixer(qt, k, vt, kmean, bias, rkv, lora, w_lora_b, a_lora_b, g_lora_b, w0, a0, k_k, k_a, r_k, ln_x_w, ln_x_b,
           *, batch, seq):
    n, wm = k.shape
    blk = MOBA_BLOCK
    nb = seq // blk
    half = nb // 2
    grp = MOBA_GROUP
    topk = min(MOBA_TOPK, max(nb - 1, 1))
    ts = TOKEN_TILE
    nt = seq // ts
    assert nb == 4 * grp and NEAR_TILES == grp + 1 and nb + 1 <= LANES and nt == half
    k3 = k.reshape(n // blk, blk, wm)
    qt4 = qt.reshape(batch, nb, wm, blk)
    vt4 = vt.reshape(batch, nb, wm, blk)
    km2 = kmean.reshape(n // blk, wm)
    hot = np.zeros((nb + 1, blk, LANES), np.float32)
    hot[np.arange(nb + 1), :, np.arange(nb + 1)] = 1.0

    wr = rkv.shape[1] // 3
    cw, ca, cg = w_lora_b.shape[0], a_lora_b.shape[0], g_lora_b.shape[0]
    lora_pad = lora.shape[1]
    assert cw == HEAD_DIM and ca == HEAD_DIM and cw + ca == LANES and wr == wm
    wlb = jnp.concatenate([w_lora_b, jnp.zeros((LANES - cw, wr), F32)], axis=0).astype(BF16)
    alb = jnp.concatenate([jnp.zeros((LANES - ca, wr), F32), a_lora_b], axis=0).astype(BF16)
    glb = jnp.concatenate([g_lora_b, jnp.zeros((lora_pad - LANES - cg, wr), F32)], axis=0).astype(BF16)
    par = jnp.stack([w0, a0, k_k, k_a, r_k.reshape(-1), ln_x_w, ln_x_b, jnp.zeros_like(w0)], axis=0)
    tok = lambda col: (lambda b, t: (b * nt + t, col))
    return pl.pallas_call(
        functools.partial(_mixer_kernel, topk=topk),
        grid=(batch, nt),
        in_specs=[
            pl.BlockSpec((1, 1, wm, blk), lambda b, t: (b, t, 0, 0)),
            pl.BlockSpec((1, 1, wm, blk), lambda b, t: (b, nb - 1 - t, 0, 0)),
            pl.BlockSpec((nb, blk, wm), lambda b, t: (b, 0, 0), pipeline_mode=pl.Buffered(1)),
            pl.BlockSpec((1, nb, wm, blk), lambda b, t: (b, 0, 0, 0), pipeline_mode=pl.Buffered(1)),
            pl.BlockSpec((nb, wm), lambda b, t: (b, 0)),
            _resident(bias.shape),
            _resident((nb + 1, blk, LANES)),
            pl.BlockSpec((ts, wr), tok(0)),
            pl.BlockSpec((ts, wr), tok(1)),
            pl.BlockSpec((ts, wr), tok(2)),
            pl.BlockSpec((ts, lora_pad), tok(0)),
            _resident((LANES, wr)), _resident((LANES, wr)), _resident((lora_pad - LANES, wr)), _resident((8, wr)),
        ],
        out_specs=[
            pl.BlockSpec((2 * blk, wm), lambda b, t: (b * half + t, 0)),
            pl.BlockSpec((ts, wr), tok(0)),
        ],
        out_shape=[jax.ShapeDtypeStruct((n, wm), BF16), jax.ShapeDtypeStruct((n, wr), BF16)],
        scratch_shapes=[pltpu.VMEM((wr // LANES, LANES, LANES), F32), pltpu.VMEM((ts, wr), F32)],
        compiler_params=_params(2),
        name="mixers",
    )(qt4, qt4, k3, vt4, km2, bias, jnp.asarray(hot, BF16), rkv, rkv, rkv, lora, wlb, alb, glb, par)


def _merge_kernel(x_ref, ya0_ref, ya1_ref, yb_ref, gate_ref, pa_ref, pb_ref, wo_ref, o_ref):
    d = x_ref.shape[1]
    gate = gate_ref[...].astype(F32)
    ya = jnp.concatenate([ya0_ref[...], ya1_ref[...]], axis=0)
    merged = (jax.nn.sigmoid(gate[:, :d]) * _dot(ya, pa_ref[...])
              + jax.nn.sigmoid(gate[:, d:]) * _dot(yb_ref[...], pb_ref[...]))
    o_ref[...] = x_ref[...] + _dot(merged.astype(BF16), wo_ref[...])


def _merge(x2d, ya, yb, gate, proj_a, proj_b, w_out, *, seq):
    n, d = x2d.shape
    wm = ya.shape[1]
    tm = 2 * MOBA_BLOCK
    nb = seq // MOBA_BLOCK
    tiles_per_seq = seq // tm
    row = lambda i: (i, 0)

    def ya_block(which):
        def index(i):
            blk_i = 2 * (i % tiles_per_seq) + which
            pos = jnp.where(blk_i < nb // 2, 2 * blk_i, 2 * (nb - 1 - blk_i) + 1)
            return ((i // tiles_per_seq) * nb + pos, 0)
        return pl.BlockSpec((MOBA_BLOCK, wm), index)

    return pl.pallas_call(
        _merge_kernel,
        grid=(n // tm,),
        in_specs=[
            pl.BlockSpec((tm, d), row), ya_block(0), ya_block(1), pl.BlockSpec((tm, wm), row),
            pl.BlockSpec((tm, 2 * d), row),
            _resident((wm, d)), _resident((wm, d)), _resident((d, d)),
        ],
        out_specs=pl.BlockSpec((tm, d), row),
        out_shape=jax.ShapeDtypeStruct((n, d), F32),
        compiler_params=_params(1),
        name="merge",
    )(x2d, ya, ya, yb, gate, proj_a.astype(BF16), proj_b.astype(BF16), w_out.astype(BF16))


def kernel(x, rel_bias, norm_ffn1, ffn1_gate, ffn1_up, ffn1_down, norm_mix, w_in, mix_rkv, mix_wag, w_lora_a, w_lora_b, w0, a_lora_a, a_lora_b, a0, g_lora_a, g_lora_b, k_k, k_a, r_k, ln_x_w, ln_x_b, proj_a, proj_b, w_out, norm_ffn2, ffn2_gate, ffn2_up, ffn2_down, norm_final):
    batch, seq, d = x.shape
    depth = norm_ffn1.shape[0]
    assert depth >= 1
    n_heads = rel_bias.shape[1]
    bias = _bias_tiles(rel_bias, n_heads)
    x2d = x.reshape(batch * seq, d)
    for l in range(depth):
        x2d = _ffn(x2d, norm_ffn1[l], ffn1_gate[l], ffn1_up[l], ffn1_down[l], norm_final, final_norm=False)
        qt, k, vt, kmean, rkv, gate, lora = _inproj(
            x2d, norm_mix[l], w_in[l], mix_rkv[l], mix_wag[l], w_lora_a[l], a_lora_a[l], g_lora_a[l],
            batch=batch, seq=seq)
        ya, yb = _mixer(qt, k, vt, kmean, bias, rkv, lora, w_lora_b[l], a_lora_b[l], g_lora_b[l], w0[l], a0[l],
                        k_k[l], k_a[l], r_k[l], ln_x_w[l], ln_x_b[l], batch=batch, seq=seq)
        x2d = _merge(x2d, ya, yb, gate, proj_a[l], proj_b[l], w_out[l], seq=seq)
        x2d = _ffn(x2d, norm_ffn2[l], ffn2_gate[l], ffn2_up[l], ffn2_down[l], norm_final,
                   final_norm=(l == depth - 1))
    return x2d.reshape(batch, seq, d)
```

```python
import functools
import math

import numpy as np
import jax
import jax.numpy as jnp
from jax import lax
from jax.experimental import pallas as pl
from jax.experimental.pallas import tpu as pltpu

F32 = jnp.float32
BF16 = jnp.bfloat16

HEAD_DIM = 64
LANES = 128
MOBA_BLOCK = 256
MOBA_TOPK = 3
MOBA_GROUP = 4
MOBA_ONES_ROWS = 16
MAX_DISTANCE = 1024
RMS_EPS = 1e-6
GN_EPS = HEAD_DIM * 1e-5
RWKV_CHUNK = 64
RWKV_WAVE_CHUNKS = 2
RWKV_WAVE_SKEW = 3
RWKV_FIELDS = ("r", "k", "v", "a", "b", "g", "bonus", "lw_hi", "lw_mid")
NEG_BIG = -1e30
LOG2E = math.log2(math.e)
NEAR_TILES = 5
TOKEN_TILE = 512
FFN_CHUNK = 256
VMEM_LIMIT = 56 * 1024 * 1024

_NT = (((1,), (1,)), ((), ()))


def _dot(a, b):
    return jnp.dot(a, b, preferred_element_type=F32)


def _dot_nt(a, b):
    return lax.dot_general(a, b, _NT, preferred_element_type=F32)


def _rms(x, gain):
    ms = jnp.mean(x * x, axis=-1, keepdims=True)
    return x * lax.rsqrt(ms + RMS_EPS) * gain


def _resident(shape):
    nd = len(shape)
    return pl.BlockSpec(shape, lambda *_: (0,) * nd, pipeline_mode=pl.Buffered(1))


def _params(n_axes):
    return pltpu.CompilerParams(dimension_semantics=("arbitrary",) * n_axes, vmem_limit_bytes=VMEM_LIMIT)


def _ffn_half_step(x, gain_ref, wg_ref, wu_ref, wd_ref, act_ref):
    fc = FFN_CHUNK
    h = _rms(x, gain_ref[...]).astype(BF16)
    for c in range(wg_ref.shape[1] // fc):
        g = _dot(h, wg_ref[:, c * fc:(c + 1) * fc])
        u = _dot(h, wu_ref[:, c * fc:(c + 1) * fc])
        act_ref[:, c * fc:(c + 1) * fc] = (g * jax.nn.sigmoid(g) * u).astype(BF16)
    return x + 0.5 * _dot(act_ref[...], wd_ref[...])


def _ffn_kernel(x_ref, gain_ref, wg_ref, wu_ref, wd_ref, o_ref, act_ref):
    o_ref[...] = _ffn_half_step(x_ref[...], gain_ref, wg_ref, wu_ref, wd_ref, act_ref)


def _ffn(x2d, gain, w_gate, w_up, w_down):
    n, d = x2d.shape
    f = w_gate.shape[1]
    tm = TOKEN_TILE
    return pl.pallas_call(
        _ffn_kernel,
        grid=(n // tm,),
        in_specs=[
            pl.BlockSpec((tm, d), lambda i: (i, 0)),
            _resident((1, d)), _resident((d, f)), _resident((d, f)), _resident((f, d)),
        ],
        out_specs=pl.BlockSpec((tm, d), lambda i: (i, 0)),
        out_shape=jax.ShapeDtypeStruct((n, d), F32),
        scratch_shapes=[pltpu.VMEM((tm, f), BF16)],
        compiler_params=_params(1),
        name="ffn",
    )(x2d, gain.reshape(1, d), w_gate.astype(BF16), w_up.astype(BF16), w_down.astype(BF16))


def _shift_rows(z, carry_row):
    rolled = pltpu.roll(z, 1, 0)
    row = lax.broadcasted_iota(jnp.int32, z.shape, 0)
    return jnp.where(row == 0, carry_row, rolled)


def _head_sum(x, lo):
    out = []
    for q in range(x.shape[1] // LANES):
        xq = x[:, q * LANES:(q + 1) * LANES]
        zero = jnp.zeros_like(xq)
        s0 = jnp.sum(jnp.where(lo, xq, zero), axis=-1, keepdims=True)
        s1 = jnp.sum(jnp.where(lo, zero, xq), axis=-1, keepdims=True)
        out.append(jnp.where(lo, s0, s1))
    return out[0] if len(out) == 1 else jnp.concatenate(out, axis=1)


def _inproj_kernel(x_ref, gain_ref, wqvt_ref, w_ref, wla_ref, mixt_ref, mixrkv_ref, wlb_ref, alb_ref, glb_ref, par_ref,
                   qt_ref, k_ref, vt_ref, kmean_ref, rw_ref, gate_ref,
                   wl_scr, carry_rkv, carry_lora, *, tiles_per_seq, lora_pad, lora_cols):
    i = pl.program_id(0)
    tm = x_ref.shape[0]
    wm = k_ref.shape[1]
    hd = HEAD_DIM
    cw, ca, cg = lora_cols

    @pl.when(i == 0)
    def _():
        col = lax.broadcasted_iota(jnp.int32, (wla_ref.shape[0], lora_pad), 1)
        m = jnp.where(col < cw, mixt_ref[:, 0:1], jnp.where(col < cw + ca, mixt_ref[:, 1:2], mixt_ref[:, 2:3]))
        wla = wla_ref[...]
        wl_scr[:, :lora_pad] = ((1.0 - m) * wla).astype(BF16)
        wl_scr[:, lora_pad:] = (m * wla).astype(BF16)

    @pl.when(i % tiles_per_seq == 0)
    def _():
        carry_rkv[...] = jnp.zeros_like(carry_rkv)
        carry_lora[...] = jnp.zeros_like(carry_lora)

    h = _rms(x_ref[...], gain_ref[...]).astype(BF16)

    kf = _dot(h, w_ref[:, wm:2 * wm])
    k_ref[...] = kf.astype(BF16)
    for j in range(tm // MOBA_BLOCK):
        kmean_ref[0, j:j + 1, :] = jnp.mean(kf[j * MOBA_BLOCK:(j + 1) * MOBA_BLOCK], axis=0, keepdims=True)

    rkv = _dot(h, w_ref[:, 3 * wm:6 * wm])
    prev = _shift_rows(rkv, carry_rkv[...])
    carry_rkv[...] = rkv[tm - 1:tm, :]
    rkv = rkv + (prev - rkv) * mixrkv_ref[...]

    gate_ref[...] = _dot(h, w_ref[:, 6 * wm:]).astype(BF16)

    lo_all = _dot(h, wl_scr[...])
    l2 = lo_all[:, lora_pad:]
    lp = lo_all[:, :lora_pad] + _shift_rows(l2, carry_lora[...])
    carry_lora[...] = l2[tm - 1:tm, :]

    lo = lax.broadcasted_iota(jnp.int32, (tm, LANES), 1) < hd
    par = par_ref[...]
    w0, a0, k_k, k_a, r_k = (par[j:j + 1, :] for j in range(5))
    wa = lp[:, :LANES]
    wa_act = jnp.where(lo, jnp.tanh(wa), wa).astype(BF16)
    lw = jax.nn.sigmoid(w0 + _dot(wa_act, wlb_ref[...])) * (-math.exp(-0.5))
    iclr = jax.nn.sigmoid(a0 + _dot(wa_act, alb_ref[...]))
    g = _dot(jax.nn.sigmoid(lp[:, LANES:]).astype(BF16), glb_ref[...])
    r = rkv[:, :wm]
    k = rkv[:, wm:2 * wm]
    v = rkv[:, 2 * wm:]
    kk = k * k_k
    kk = kk * lax.rsqrt(jnp.maximum(_head_sum(kk * kk, lo), 1e-24))
    k = k * (1.0 + (iclr - 1.0) * k_a)
    lw_hi = lw.astype(BF16)
    fields = dict(r=r, k=k, v=v, a=-kk, b=kk * iclr, g=g, bonus=_head_sum(r * k * r_k, lo) * v,
                  lw_hi=lw_hi, lw_mid=lw - lw_hi.astype(F32))
    for n, name in enumerate(RWKV_FIELDS):
        rw_ref[:, n * wm:(n + 1) * wm] = fields[name].astype(BF16)

    qvt = _dot_nt(wqvt_ref[...], h)
    for j in range(tm // MOBA_BLOCK):
        cols = slice(j * MOBA_BLOCK, (j + 1) * MOBA_BLOCK)
        qt_ref[0, j] = (qvt[:wm, cols] * (HEAD_DIM ** -0.5 * LOG2E)).astype(BF16)
        vt_ref[0, j] = qvt[wm:, cols].astype(BF16)


def _inproj(x2d, gain, w_in, mix_rkv, mix_wag, w_lora_a, a_lora_a, g_lora_a, w_lora_b, a_lora_b, g_lora_b,
            w0, a0, k_k, k_a, r_k, *, batch, seq):
    n, d = x2d.shape
    tm = TOKEN_TILE
    wm = (w_in.shape[1] - 2 * d) // 6
    nblk = tm // MOBA_BLOCK
    cw, ca, cg = w_lora_a.shape[1], a_lora_a.shape[1], g_lora_a.shape[1]
    assert cw == HEAD_DIM and ca == HEAD_DIM and cw + ca == LANES
    lora_pad = -(-(cw + ca + cg) // LANES) * LANES
    wb = w_in.astype(BF16)
    wqvt = jnp.concatenate([wb[:, :wm], wb[:, 2 * wm:3 * wm]], axis=1).T
    wla = jnp.concatenate([w_lora_a, a_lora_a, g_lora_a, jnp.zeros((d, lora_pad - cw - ca - cg), F32)], axis=1)
    wlb = jnp.concatenate([w_lora_b, jnp.zeros((LANES - cw, wm), F32)], axis=0).astype(BF16)
    alb = jnp.concatenate([jnp.zeros((LANES - ca, wm), F32), a_lora_b], axis=0).astype(BF16)
    glb = jnp.concatenate([g_lora_b, jnp.zeros((lora_pad - LANES - cg, wm), F32)], axis=0).astype(BF16)
    zero = jnp.zeros_like(w0)
    par = jnp.stack([w0, a0, k_k, k_a, r_k.reshape(-1), zero, zero, zero], axis=0)
    kern = functools.partial(_inproj_kernel, tiles_per_seq=seq // tm, lora_pad=lora_pad, lora_cols=(cw, ca, cg))
    row = lambda i: (i, 0)
    return pl.pallas_call(
        kern,
        grid=(n // tm,),
        in_specs=[
            pl.BlockSpec((tm, d), row),
            _resident((1, d)),
            _resident((2 * wm, d)), _resident(w_in.shape),
            _resident((d, lora_pad)), _resident((d, 3)), _resident((1, 3 * wm)),
            _resident((LANES, wm)), _resident((LANES, wm)), _resident((lora_pad - LANES, wm)), _resident((8, wm)),
        ],
        out_specs=[
            pl.BlockSpec((1, nblk, wm, MOBA_BLOCK), lambda i: (i, 0, 0, 0)),
            pl.BlockSpec((tm, wm), row),
            pl.BlockSpec((1, nblk, wm, MOBA_BLOCK), lambda i: (i, 0, 0, 0)),
            pl.BlockSpec((1, nblk, wm), lambda i: (i, 0, 0)),
            pl.BlockSpec((tm, len(RWKV_FIELDS) * wm), row),
            pl.BlockSpec((tm, 2 * d), row),
        ],
        out_shape=[
            jax.ShapeDtypeStruct((n // tm, nblk, wm, MOBA_BLOCK), BF16),
            jax.ShapeDtypeStruct((n, wm), BF16),
            jax.ShapeDtypeStruct((n // tm, nblk, wm, MOBA_BLOCK), BF16),
            jax.ShapeDtypeStruct((n // tm, nblk, wm), F32),
            jax.ShapeDtypeStruct((n, len(RWKV_FIELDS) * wm), BF16),
            jax.ShapeDtypeStruct((n, 2 * d), BF16),
        ],
        scratch_shapes=[
            pltpu.VMEM((d, 2 * lora_pad), BF16),
            pltpu.VMEM((1, 3 * wm), F32),
            pltpu.VMEM((1, lora_pad), F32),
        ],
        compiler_params=_params(1),
        name="inproj",
    )(x2d, gain.reshape(1, d), wqvt, wb, wla, mix_wag.T, mix_rkv.reshape(1, 3 * wm), wlb, alb, glb, par)


def _t5_bucket_np(dist, num_buckets):
    n = np.maximum(dist, 0)
    max_exact = num_buckets // 2
    nf = np.maximum(n, max_exact).astype(np.float32)
    large = max_exact + (np.log(nf / np.float32(max_exact)) / np.float32(math.log(MAX_DISTANCE / max_exact))
                         * np.float32(num_buckets - max_exact)).astype(np.int32)
    large = np.minimum(large, num_buckets - 1)
    return np.where(n < max_exact, n, large).astype(np.int32)


def _bias_kernel(rb_ref, bkt_ref, o_ref, *, num_buckets, bucket_ranges):
    h = pl.program_id(0)
    last = rb_ref[num_buckets - 1, h]
    blk = bkt_ref.shape[1]
    causal = lax.broadcasted_iota(jnp.int32, (blk, blk), 0) <= lax.broadcasted_iota(jnp.int32, (blk, blk), 1)
    for o, (b_lo, b_hi) in enumerate(bucket_ranges):
        bkt = bkt_ref[o]
        acc = jnp.zeros(bkt.shape, F32)
        for b in range(b_lo, min(b_hi, num_buckets - 2) + 1):
            acc = jnp.where(bkt == b, rb_ref[b, h] - last, acc)
        acc = acc * LOG2E
        o_ref[0, o] = jnp.where(causal, acc, NEG_BIG) if o == 0 else acc


def _bias_tiles(rel_bias, n_heads):
    num_buckets = rel_bias.shape[0]
    blk = MOBA_BLOCK
    key = np.arange(blk)[:, None]
    qry = np.arange(blk)[None, :]
    offs = np.arange(NEAR_TILES)[:, None, None]
    bkt = _t5_bucket_np(offs * blk + qry - key, num_buckets)
    far = _t5_bucket_np(np.arange(NEAR_TILES * blk - blk + 1, 16 * blk * 64), num_buckets)
    assert (far == num_buckets - 1).all()
    ranges = tuple((int(bkt[o].min()), int(bkt[o].max())) for o in range(NEAR_TILES))
    return pl.pallas_call(
        functools.partial(_bias_kernel, num_buckets=num_buckets, bucket_ranges=ranges),
        grid=(n_heads,),
        in_specs=[
            pl.BlockSpec(memory_space=pltpu.SMEM),
            _resident((NEAR_TILES, blk, blk)),
        ],
        out_specs=pl.BlockSpec((1, NEAR_TILES, blk, blk), lambda h: (h, 0, 0, 0)),
        out_shape=jax.ShapeDtypeStruct((n_heads, NEAR_TILES, blk, blk), F32),
        compiler_params=_params(1),
        name="bias_tiles",
    )(rel_bias, jnp.asarray(bkt))


def _moba_kernel(qa_ref, qb_ref, k_ref, vt_ref, kmean_ref, bias_ref, hot_ref, o_ref, *, topk):
    p = pl.program_id(2)
    blk = MOBA_BLOCK
    nb = k_ref.shape[0]
    hd = HEAD_DIM
    grp = MOBA_GROUP
    i_a = p
    i_b = nb - 1 - p

    chan = lax.broadcasted_iota(jnp.int32, (LANES, blk), 0)
    head_rows = [chan < hd, chan >= hd]
    km = kmean_ref[...]
    km_hi = km.astype(BF16)
    km_lo = (km - km_hi.astype(F32)).astype(BF16)
    nidx = lax.broadcasted_iota(jnp.int32, (nb, blk), 0)
    ones_rows = jnp.ones((MOBA_ONES_ROWS, blk), BF16)
    dead_rows = jnp.full((LANES - nb, blk), NEG_BIG, BF16)

    def query_operands(qt_ref, i):
        qt2 = qt_ref[0, 0]
        out = []
        for e in range(2):
            qt_e = jnp.where(head_rows[e], qt2, jnp.zeros_like(qt2))
            gate = _dot(km_hi, qt_e) + _dot(km_lo, qt_e)
            valid = nidx < i
            gv = jnp.where(valid, gate, -jnp.inf)
            cnt = jnp.zeros((nb, blk), jnp.int32)
            for n2 in range(nb):
                row = gv[n2:n2 + 1, :]
                beats = (row > gv) | ((row == gv) & (n2 < nidx))
                cnt = cnt + beats.astype(jnp.int32)
            sel = valid & (cnt < topk)
            msk = jnp.where(sel, 0.0, NEG_BIG).astype(BF16)
            out.append((qt_e, jnp.concatenate([qt_e, msk, dead_rows], axis=0)))
        return out

    def scores(tile):
        who, qops, j, bias_idx = tile
        if bias_idx == 0:
            s = [_dot(k_ref[j], qops[e][0]) for e in range(2)]
        else:
            k_aug = jnp.concatenate([k_ref[jnp.maximum(j, 0)], hot_ref[jnp.where(j >= 0, j, nb)]], axis=1)
            s = [_dot(k_aug, qops[e][1]) for e in range(2)]
        if bias_idx is not None:
            s = [s[e] + bias_ref[e, bias_idx] for e in range(2)]
        return s

    def update(st, e, s, j):
        m, acc = st
        m_new = jnp.maximum(m, jnp.max(s, axis=0, keepdims=True))
        vt_t = jnp.concatenate([vt_ref[0, jnp.maximum(j, 0), e * hd:(e + 1) * hd, :], ones_rows], axis=0)
        return m_new, jnp.exp2(m - m_new) * acc + _dot(vt_t, jnp.exp2(s - m_new).astype(BF16))

    def past_tiles(who, qops, i, g, near):
        return [(who, qops, i - grp * (g + 1) + t, (grp - t) if near else None) for t in range(grp)]

    qa = query_operands(qa_ref, i_a)
    qb = query_operands(qb_ref, i_b)
    a_turn = i_a > grp
    qx = [(None, jnp.where(a_turn, qa[e][1], qb[e][1])) for e in range(2)]
    tiles = ([("b", qb, i_b, 0), ("a", qa, i_a, 0)]
             + past_tiles("b", qb, i_b, 0, True) + past_tiles("b", qb, i_b, 1, False)
             + past_tiles("a", qa, i_a, 0, True) + past_tiles("b", qb, i_b, 2, False)
             + past_tiles("x", qx, jnp.where(a_turn, i_a, i_b), jnp.where(a_turn, 1, 3), False))

    init = (jnp.full((1, blk), NEG_BIG, F32), jnp.zeros((hd + MOBA_ONES_ROWS, blk), F32))
    st = {"a": [init, init], "b": [init, init]}
    ahead = 4
    pending = [scores(t) for t in tiles[:ahead]]
    for n, tile in enumerate(tiles):
        if n + ahead < len(tiles):
            pending.append(scores(tiles[n + ahead]))
        s = pending.pop(0)
        who, _, j, _ = tile
        if who == "x" and "x" not in st:
            st["x"] = [tuple(jnp.where(a_turn, ua, ub) for ua, ub in zip(st["a"][e], st["b"][e])) for e in range(2)]
        st[who] = [update(st[who][e], e, s[e], j) for e in range(2)]
    st_a = [tuple(jnp.where(a_turn, ux, ua) for ux, ua in zip(st["x"][e], st["a"][e])) for e in range(2)]
    st_b = [tuple(jnp.where(a_turn, ub, ux) for ux, ub in zip(st["x"][e], st["b"][e])) for e in range(2)]

    for half, sth in enumerate((st_a, st_b)):
        out_t = jnp.concatenate([sth[e][1][:hd] / sth[e][1][hd:hd + 1] for e in range(2)], axis=0)
        o_ref[half * blk:(half + 1) * blk, :] = out_t.T.astype(BF16)


def _moba(qt, k, vt, kmean, bias, *, batch, seq):
    n, wm = k.shape
    blk = MOBA_BLOCK
    nb = seq // blk
    n_pairs = wm // LANES
    grp = MOBA_GROUP
    topk = min(MOBA_TOPK, max(nb - 1, 1))
    assert nb == 4 * grp and NEAR_TILES == grp + 1 and nb + 1 <= LANES
    k3 = k.reshape(n // blk, blk, wm)
    qt4 = qt.reshape(batch, nb, wm, blk)
    vt4 = vt.reshape(batch, nb, wm, blk)
    km2 = kmean.reshape(n // blk, wm)
    hot = np.zeros((nb + 1, blk, LANES), np.float32)
    hot[np.arange(nb + 1), :, np.arange(nb + 1)] = 1.0
    half = nb // 2
    return pl.pallas_call(
        functools.partial(_moba_kernel, topk=topk),
        grid=(batch, n_pairs, half),
        in_specs=[
            pl.BlockSpec((1, 1, LANES, blk), lambda b, hp, p: (b, p, hp, 0)),
            pl.BlockSpec((1, 1, LANES, blk), lambda b, hp, p: (b, nb - 1 - p, hp, 0)),
            pl.BlockSpec((nb, blk, LANES), lambda b, hp, p: (b, 0, hp)),
            pl.BlockSpec((1, nb, LANES, blk), lambda b, hp, p: (b, 0, hp, 0)),
            pl.BlockSpec((nb, LANES), lambda b, hp, p: (b, hp)),
            pl.BlockSpec((2, NEAR_TILES, blk, blk), lambda b, hp, p: (hp, 0, 0, 0)),
            _resident((nb + 1, blk, LANES)),
        ],
        out_specs=pl.BlockSpec((2 * blk, LANES), lambda b, hp, p: (b * half + p, hp)),
        out_shape=jax.ShapeDtypeStruct((n, wm), BF16),
        compiler_params=_params(3),
        name="moba",
    )(qt4, qt4, k3, vt4, km2, bias, jnp.asarray(hot, BF16))


def _rwkv_kernel(*refs):
    nf = len(RWKV_FIELDS)
    fld = dict(zip(RWKV_FIELDS, refs[:nf]))
    par_ref, o_ref, z_scr, y_scr = refs[nf:]
    t = pl.program_id(1)
    ts, width = o_ref.shape
    n_pairs = width // LANES
    c = RWKV_CHUNK
    hd = HEAD_DIM

    @pl.when(t == 0)
    def _():
        z_scr[...] = jnp.zeros_like(z_scr)

    lo = lax.broadcasted_iota(jnp.int32, (ts, LANES), 1) < hd
    par = par_ref[...]
    ln_w, ln_b = par[0:1, :], par[1:2, :]

    row_c = lax.broadcasted_iota(jnp.int32, (c, c), 0)
    col_c = lax.broadcasted_iota(jnp.int32, (c, c), 1)
    tri = jnp.where(row_c >= col_c, 1.0, 0.0).astype(BF16)
    eye = jnp.where(row_c == col_c, 1.0, 0.0)
    row_p = lax.broadcasted_iota(jnp.int32, (LANES, LANES), 0)
    col_p = lax.broadcasted_iota(jnp.int32, (LANES, LANES), 1)
    same_head = (row_p < hd) == (col_p < hd)
    diag_p = row_p == col_p
    lo_ar = lax.broadcasted_iota(jnp.int32, (2 * c, LANES), 1) < hd

    nc = ts // c
    row_w = lax.broadcasted_iota(jnp.int32, (c, LANES), 0)
    col_w = lax.broadcasted_iota(jnp.int32, (c, LANES), 1) & (c - 1)
    strict2 = row_w > col_w
    incl2 = row_w >= col_w
    zeros_cb = jnp.zeros((c, LANES), BF16)
    lo_c2 = (lax.broadcasted_iota(jnp.int32, (c, 2 * LANES), 1) & (LANES - 1)) < hd
    z = [z_scr[q] for q in range(n_pairs)]

    def rows(ci):
        return slice(ci * c, (ci + 1) * c)

    def wave(chunk_ids):
        chunks = [(q, ci) for ci in chunk_ids for q in range(n_pairs)]
        units = [(q, ci, e) for (q, ci) in chunks for e in range(2)]

        cums = {ci: _dot(tri, fld["lw_hi"][rows(ci), :]) + _dot(tri, fld["lw_mid"][rows(ci), :]) for ci in chunk_ids}
        yield
        ch = {}
        for (q, ci) in chunks:
            sl = (rows(ci), slice(q * LANES, (q + 1) * LANES))
            rc, kc, ac, bc = (fld[name][sl].astype(F32) for name in ("r", "k", "a", "b"))
            vb = fld["v"][sl]
            lwc = fld["lw_hi"][sl].astype(F32) + fld["lw_mid"][sl].astype(F32)
            cum = cums[ci][:, sl[1]]
            tot = cum[c - 1:c, :]
            e_neg = jnp.exp(-cum)
            e_fwd = jnp.exp(tot - cum)
            rt = rc * jnp.exp(cum)
            at = ac * jnp.exp(cum - lwc)
            ch[q, ci] = dict(
                rt=rt, at=at, p_c=jnp.exp(tot),
                ar=jnp.concatenate([at, rt], axis=0),
                btkt=jnp.concatenate([bc * e_neg, kc * e_neg], axis=0).astype(BF16),
                bbkb=jnp.concatenate([bc * e_fwd, kc * e_fwd], axis=0),
                vpad=jnp.concatenate([zeros_cb, vb], axis=0),
                zv=jnp.concatenate([zeros_cb, vb], axis=1),
            )

        dtop, dbot = {}, {}
        for (q, ci, e) in units:
            ar_e = jnp.where(lo_ar if e == 0 else ~lo_ar, ch[q, ci]["ar"], 0.0).astype(BF16)
            d = _dot_nt(ar_e, ch[q, ci]["btkt"])
            dtop[q, ci, e] = jnp.where(strict2, d[:c], 0.0)
            dbot[q, ci, e] = jnp.where(incl2, d[c:], 0.0).astype(BF16)
        yield

        pw = {u: dtop[u][:, :c] for u in units}
        tm = {u: eye + pw[u] for u in units}
        for u in units:
            pb = pw[u].astype(BF16)
            pw[u] = _dot(pb, pb)
        yield
        for _ in range(int(math.log2(c)) - 2):
            for u in units:
                pb = pw[u].astype(BF16)
                sq = _dot(jnp.concatenate([pb, tm[u].astype(BF16)], axis=0), pb)
                pw[u] = sq[:c]
                tm[u] = tm[u] + sq[c:]
            yield
        akv = {}
        for (q, ci, e) in units:
            tm[q, ci, e] = tm[q, ci, e] + _dot(tm[q, ci, e].astype(BF16), pw[q, ci, e].astype(BF16))
            akv[q, ci, e] = _dot(dtop[q, ci, e].astype(BF16), ch[q, ci]["vpad"])
        yield
        wu = {(q, ci, e): _dot(tm[q, ci, e].astype(BF16),
                               jnp.concatenate([ch[q, ci]["at"], akv[q, ci, e]], axis=1).astype(BF16))
              for (q, ci, e) in units}
        yield
        qy = {(q, ci, e): _dot(dbot[q, ci, e], jnp.concatenate([wu[q, ci, e].astype(BF16), ch[q, ci]["zv"]], axis=0))
              for (q, ci, e) in units}
        yield

        m2s, g2s, qes, y0s = {}, {}, {}, {}
        for (q, ci) in chunks:
            wu2 = jnp.where(lo_c2, wu[q, ci, 0], wu[q, ci, 1])
            qy2 = jnp.where(lo_c2, qy[q, ci, 0], qy[q, ci, 1])
            bkt = ch[q, ci]["bbkb"].T.astype(BF16)
            rhs = jnp.concatenate([wu2.astype(BF16), ch[q, ci]["zv"]], axis=0)
            mg = _dot(bkt, rhs)
            m2s[q, ci] = (jnp.where(same_head, mg[:, :LANES], 0.0)
                          + jnp.where(diag_p, ch[q, ci]["p_c"], 0.0)).astype(BF16)
            g2s[q, ci] = jnp.where(same_head, mg[:, LANES:], 0.0)
            qes[q, ci] = (ch[q, ci]["rt"] + qy2[:, :LANES]).astype(BF16)
            y0s[q, ci] = qy2[:, LANES:]
        yield

        for ci in chunk_ids:
            for q in range(n_pairs):
                zb = z[q].astype(BF16)
                y_scr[rows(ci), q * LANES:(q + 1) * LANES] = _dot(qes[q, ci], zb) + y0s[q, ci]
                z[q] = _dot(m2s[q, ci], zb) + g2s[q, ci]
            yield

    per_wave = RWKV_WAVE_CHUNKS
    waves = [wave(list(range(w, w + per_wave))) for w in range(0, nc, per_wave)]
    started, live = 0, []
    tick = 0
    while started < len(waves) or live:
        if started < len(waves) and tick % RWKV_WAVE_SKEW == 0:
            live.append(waves[started])
            started += 1
        for g_ in list(live):
            try:
                next(g_)
            except StopIteration:
                live.remove(g_)
        tick += 1
    for q in range(n_pairs):
        z_scr[q] = z[q]

    y = y_scr[...]
    mu = _head_sum(y, lo) * (1.0 / hd)
    dy = y - mu
    var = _head_sum(dy * dy, lo) * (1.0 / hd)
    yn = dy * lax.rsqrt(var + GN_EPS) * ln_w + ln_b
    o_ref[...] = ((yn + fld["bonus"][...].astype(F32)) * fld["g"][...].astype(F32)).astype(BF16)


def _rwkv(rw, ln_x_w, ln_x_b, *, batch, seq):
    n = rw.shape[0]
    wr = rw.shape[1] // len(RWKV_FIELDS)
    ts = TOKEN_TILE
    nt = seq // ts
    zero = jnp.zeros_like(ln_x_w)
    par = jnp.stack([ln_x_w, ln_x_b] + [zero] * 6, axis=0)
    field = lambda n_: pl.BlockSpec((ts, wr), lambda b, t: (b * nt + t, n_))
    return pl.pallas_call(
        _rwkv_kernel,
        grid=(batch, nt),
        in_specs=[field(n_) for n_ in range(len(RWKV_FIELDS))] + [_resident((8, wr))],
        out_specs=pl.BlockSpec((ts, wr), lambda b, t: (b * nt + t, 0)),
        out_shape=jax.ShapeDtypeStruct((n, wr), BF16),
        scratch_shapes=[pltpu.VMEM((wr // LANES, LANES, LANES), F32), pltpu.VMEM((ts, wr), F32)],
        compiler_params=_params(2),
        name="rwkv7",
    )(*([rw] * len(RWKV_FIELDS)), par)


def _merge_ffn_kernel(x_ref, ya0_ref, ya1_ref, yb_ref, gate_ref, pa_ref, pb_ref, wo_ref,
                      gain_ref, wg_ref, wu_ref, wd_ref, fgain_ref, o_ref, act_ref, *, final_norm):
    d = x_ref.shape[1]
    gate = gate_ref[...].astype(F32)
    ya = jnp.concatenate([ya0_ref[...], ya1_ref[...]], axis=0)
    merged = (jax.nn.sigmoid(gate[:, :d]) * _dot(ya, pa_ref[...])
              + jax.nn.sigmoid(gate[:, d:]) * _dot(yb_ref[...], pb_ref[...]))
    x = x_ref[...] + _dot(merged.astype(BF16), wo_ref[...])
    o = _ffn_half_step(x, gain_ref, wg_ref, wu_ref, wd_ref, act_ref)
    if final_norm:
        o = _rms(o, fgain_ref[...])
    o_ref[...] = o


def _merge_ffn(x2d, ya, yb, gate, proj_a, proj_b, w_out, gain, w_gate, w_up, w_down, final_gain, *, seq, final_norm):
    n, d = x2d.shape
    wm = ya.shape[1]
    f = w_gate.shape[1]
    tm = 2 * MOBA_BLOCK
    nb = seq // MOBA_BLOCK
    tiles_per_seq = seq // tm
    row = lambda i: (i, 0)

    def ya_block(which):
        def index(i):
            blk_i = 2 * (i % tiles_per_seq) + which
            pos = jnp.where(blk_i < nb // 2, 2 * blk_i, 2 * (nb - 1 - blk_i) + 1)
            return ((i // tiles_per_seq) * nb + pos, 0)
        return pl.BlockSpec((MOBA_BLOCK, wm), index)

    return pl.pallas_call(
        functools.partial(_merge_ffn_kernel, final_norm=final_norm),
        grid=(n // tm,),
        in_specs=[
            pl.BlockSpec((tm, d), row), ya_block(0), ya_block(1), pl.BlockSpec((tm, wm), row),
            pl.BlockSpec((tm, 2 * d), row),
            _resident((wm, d)), _resident((wm, d)), _resident((d, d)),
            _resident((1, d)), _resident((d, f)), _resident((d, f)), _resident((f, d)), _resident((1, d)),
        ],
        out_specs=pl.BlockSpec((tm, d), row),
        out_shape=jax.ShapeDtypeStruct((n, d), F32),
        scratch_shapes=[pltpu.VMEM((tm, f), BF16)],
        compiler_params=_params(1),
        name="merge_ffn",
    )(x2d, ya, ya, yb, gate, proj_a.astype(BF16), proj_b.astype(BF16), w_out.astype(BF16),
      gain.reshape(1, d), w_gate.astype(BF16), w_up.astype(BF16), w_down.astype(BF16), final_gain.reshape(1, d))


def kernel(x, rel_bias, norm_ffn1, ffn1_gate, ffn1_up, ffn1_down, norm_mix, w_in, mix_rkv, mix_wag, w_lora_a, w_lora_b, w0, a_lora_a, a_lora_b, a0, g_lora_a, g_lora_b, k_k, k_a, r_k, ln_x_w, ln_x_b, proj_a, proj_b, w_out, norm_ffn2, ffn2_gate, ffn2_up, ffn2_down, norm_final):
    batch, seq, d = x.shape
    depth = norm_ffn1.shape[0]
    assert depth >= 1
    n_heads = rel_bias.shape[1]
    bias = _bias_tiles(rel_bias, n_heads)
    x2d = x.reshape(batch * seq, d)
    for l in range(depth):
        x2d = _ffn(x2d, norm_ffn1[l], ffn1_gate[l], ffn1_up[l], ffn1_down[l])
        qt, k, vt, kmean, rw, gate = _inproj(
            x2d, norm_mix[l], w_in[l], mix_rkv[l], mix_wag[l], w_lora_a[l], a_lora_a[l], g_lora_a[l],
            w_lora_b[l], a_lora_b[l], g_lora_b[l], w0[l], a0[l], k_k[l], k_a[l], r_k[l], batch=batch, seq=seq)
        ya = _moba(qt, k, vt, kmean, bias, batch=batch, seq=seq)
        yb = _rwkv(rw, ln_x_w[l], ln_x_b[l], batch=batch, seq=seq)
        x2d = _merge_ffn(x2d, ya, yb, gate, proj_a[l], proj_b[l], w_out[l], norm_ffn2[l], ffn2_gate[l], ffn2_up[l],
                         ffn2_down[l], norm_final, seq=seq, final_norm=(l == depth - 1))
    return x2d.reshape(batch, seq, d)
```
